```python
import math
import jax, jax.numpy as jnp
from jax import lax
import numpy as np

D_MODEL = 1024
BATCH = 8
SEQ = 8192
DEPTH = 1
DEC_BATCH = 128
DEC_SEQ = 4
PAST_LEN = 8192
PAGE_SIZE = 128

HEAD_DIM = 64
N_HEADS = D_MODEL // HEAD_DIM
DSA_HEADS = N_HEADS // 2
NSA_HEADS = N_HEADS - DSA_HEADS
DSA_KV = 2
NSA_KV = 2
DSA_GRP = DSA_HEADS // DSA_KV
NSA_GRP = NSA_HEADS // NSA_KV
IDX_HEADS = 8
IDX_DIM = 64
IDX_TOPK = 256
CMP_BLOCK = 32
SEL_BLOCK = 64
SEL_RATIO = SEL_BLOCK // CMP_BLOCK
N_SEL = 16
WINDOW = 512
NUM_BUCKETS = 32
MAX_EXACT = NUM_BUCKETS // 2
MAX_DISTANCE = 128
D_FF = ((8 * D_MODEL // 3 + 127) // 128) * 128
Q_BLOCK = 128
EPS = 1e-6
NEG = -1e30
BIG = 1e30
F32 = jnp.float32

IN_SIZES = (DSA_HEADS * HEAD_DIM, 2 * DSA_KV * HEAD_DIM, IDX_HEADS * IDX_DIM, IDX_HEADS, IDX_DIM,
            NSA_HEADS * HEAD_DIM, 2 * NSA_KV * HEAD_DIM, 2 * NSA_KV * HEAD_DIM, 2 * NSA_KV * HEAD_DIM,
            3 * NSA_HEADS, 2 * D_MODEL)
D_IN = ((DSA_HEADS + NSA_HEADS) * HEAD_DIM + 2 * DSA_KV * HEAD_DIM + IDX_HEADS * IDX_DIM + IDX_HEADS
        + IDX_DIM + 6 * NSA_KV * HEAD_DIM + 3 * NSA_HEADS + 2 * D_MODEL)

kernel_name = 'dsa_nsa_gated_hybrid_decode_step'


def rmsnorm(x, g):
    xf = x.astype(F32)
    y = xf * lax.rsqrt(jnp.mean(xf * xf, axis=-1, keepdims=True) + EPS)
    return (y * g.astype(F32)).astype(x.dtype)


def swiglu(x, g, w_gu, w_dn):
    a, b = jnp.split(rmsnorm(x, g) @ w_gu, 2, axis=-1)
    return (jax.nn.silu(a) * b) @ w_dn


def t5_bucket(dist):
    n = jnp.maximum(dist, 0)
    nf = jnp.maximum(n, 1).astype(F32)
    log_ratio = jnp.log(nf / MAX_EXACT) / math.log(MAX_DISTANCE / MAX_EXACT)
    large = MAX_EXACT + (log_ratio * (NUM_BUCKETS - MAX_EXACT)).astype(jnp.int32)
    return jnp.where(n < MAX_EXACT, n, jnp.minimum(large, NUM_BUCKETS - 1))


def kv_rows(a, n_kv, g_k):
    B, T = a.shape[:2]
    a = a.reshape(B, T, 2, n_kv, HEAD_DIM)
    return jnp.stack([rmsnorm(a[:, :, 0], g_k), a[:, :, 1]], axis=2)


def mixer_inputs(h, g_mix, w_in, dsa_q_norm, dsa_k_norm, idx_k_norm, nsa_q_norm, nsa_sel_k_norm, nsa_win_k_norm):
    B, T = h.shape[:2]
    z = rmsnorm(h, g_mix) @ w_in
    splits = np.cumsum(IN_SIZES)[:-1].tolist()
    dq, dkv, iq, iw, ik, nq, ckv, skv, wkv, ng, mg = jnp.split(z, splits, axis=-1)
    gate_a, gate_b = jnp.split(jax.nn.sigmoid(mg), 2, axis=-1)
    return {
        'dsa_q': rmsnorm(dq.reshape(B, T, DSA_KV, DSA_GRP, HEAD_DIM), dsa_q_norm),
        'dsa_kv': kv_rows(dkv, DSA_KV, dsa_k_norm),
        'idx_q': iq.reshape(B, T, IDX_HEADS, IDX_DIM) * IDX_DIM ** -0.5,
        'idx_w': iw * IDX_HEADS ** -0.5,
        'idx_k': rmsnorm(ik, idx_k_norm),
        'nsa_q': rmsnorm(nq.reshape(B, T, NSA_KV, NSA_GRP, HEAD_DIM), nsa_q_norm),
        'cmp_kv': ckv.reshape(B, T, 2, NSA_KV, HEAD_DIM),
        'sel_kv': kv_rows(skv, NSA_KV, nsa_sel_k_norm),
        'win_kv': kv_rows(wkv, NSA_KV, nsa_win_k_norm),
        'nsa_gate': jax.nn.sigmoid(ng).reshape(B, T, NSA_KV, NSA_GRP, 3),
        'gate_a': gate_a,
        'gate_b': gate_b,
    }


def indexer_topk(iq, iw, ik, t_pos, s_pos, k):
    s = jnp.einsum('bqhd,bsd->bqhs', iq, ik).astype(F32)
    score = jnp.einsum('bqhs,bqh->bqs', jax.nn.relu(s), iw.astype(F32))
    score = jnp.where(s_pos[None, None, :] <= t_pos[None, :, None], score, NEG)
    _, idx = lax.top_k(score, k)
    return idx


def dsa_attend(q, t_pos, k, v, s_pos, bias_tab):
    dist = t_pos[None, :, None] - s_pos
    bias = bias_tab.astype(F32)[t5_bucket(dist)]
    bias = bias.reshape(dist.shape + (DSA_KV, DSA_GRP)).transpose(0, 1, 3, 4, 2)
    logits = jnp.einsum('bqgrd,bqngd->bqgrn', q, k).astype(F32) * HEAD_DIM ** -0.5 + bias
    logits = jnp.where((dist >= 0)[:, :, None, None, :], logits, NEG)
    p = jax.nn.softmax(logits, axis=-1).astype(v.dtype)
    return jnp.einsum('bqgrn,bqngd->bqgrd', p, v)


def dsa_prompt(m, bias_tab):
    q, iq, iw, ik, kv = m['dsa_q'], m['idx_q'], m['idx_w'], m['idx_k'], m['dsa_kv']
    B, S = q.shape[:2]
    k_top = min(IDX_TOPK, S // 4)
    s_pos = jnp.arange(S)
    b_idx = jnp.arange(B)[:, None, None]

    def block(i):
        st = i * Q_BLOCK
        t_pos = st + jnp.arange(Q_BLOCK)
        qb = lax.dynamic_slice_in_dim(q, st, Q_BLOCK, axis=1)
        iqb = lax.dynamic_slice_in_dim(iq, st, Q_BLOCK, axis=1)
        iwb = lax.dynamic_slice_in_dim(iw, st, Q_BLOCK, axis=1)
        idx = indexer_topk(iqb, iwb, ik, t_pos, s_pos, k_top)
        sel = kv[b_idx, idx]
        return dsa_attend(qb, t_pos, sel[:, :, :, 0], sel[:, :, :, 1], idx, bias_tab)

    o = lax.map(block, jnp.arange(S // Q_BLOCK))
    return jnp.moveaxis(o, 0, 1).reshape(B, S, DSA_HEADS * HEAD_DIM)


def dsa_sample(m, cache_kv, cache_ik, l, page_table, bias_tab):
    q, iq, iw, ik_new, kv_new = m['dsa_q'], m['idx_q'], m['idx_w'], m['idx_k'], m['dsa_kv']
    DB, DS = q.shape[:2]
    past = page_table.shape[1] * PAGE_SIZE
    L = past + DS
    k_top = min(IDX_TOPK, L // 4)
    t_pos = past + jnp.arange(DS)
    ik_past = cache_ik[l, page_table].reshape(DB, past, IDX_DIM)
    ik = jnp.concatenate([ik_past, ik_new], axis=1)
    idx = indexer_topk(iq, iw, ik, t_pos, jnp.arange(L), k_top)
    b_idx = jnp.arange(DB)[:, None, None]
    ip = jnp.minimum(idx, past - 1)
    page = page_table[b_idx, ip // PAGE_SIZE]
    from_pool = cache_kv[l, page, ip % PAGE_SIZE]
    from_new = kv_new[b_idx, jnp.clip(idx - past, 0, DS - 1)]
    sel = jnp.where((idx < past)[..., None, None, None], from_pool, from_new)
    o = dsa_attend(q, t_pos, sel[:, :, :, 0], sel[:, :, :, 1], idx, bias_tab)
    return o.reshape(DB, DS, DSA_HEADS * HEAD_DIM)


def compress(rows, w_phi, g_ck):
    B, T = rows.shape[:2]
    n = T // CMP_BLOCK
    r = rows[:, :n * CMP_BLOCK].reshape(B, n, CMP_BLOCK, 2, NSA_KV, HEAD_DIM)
    c = jnp.einsum('bnlcgd,clde->bncge', r, w_phi)
    return rmsnorm(c[:, :, 0], g_ck), c[:, :, 1]


def nsa_core(q, gates, t_pos, ck, cv, gather_sel, n_sblk, n_sel, wk, wv, w_pos, bias_tab):
    B, Tq = q.shape[:2]
    scale = HEAD_DIM ** -0.5
    bt = bias_tab.astype(F32).reshape(NUM_BUCKETS, NSA_KV, NSA_GRP)
    nc = ck.shape[1]
    dist_c = t_pos[:, None] - ((jnp.arange(nc) + 1) * CMP_BLOCK - 1)[None, :]
    valid_c = dist_c >= 0
    lc = jnp.einsum('bqgrd,bngd->bgrqn', q, ck).astype(F32) * scale + bt[t5_bucket(dist_c)].transpose(2, 3, 0, 1)
    pc = jax.nn.softmax(jnp.where(valid_c, lc, NEG), axis=-1) * valid_c
    o_c = jnp.einsum('bgrqn,bngd->bqgrd', pc.astype(cv.dtype), cv)
    imp = pc.sum(axis=2).transpose(0, 2, 1, 3)
    imp = jnp.pad(imp, ((0, 0), (0, 0), (0, 0), (0, n_sblk * SEL_RATIO - nc)))
    imp = imp.reshape(B, Tq, NSA_KV, n_sblk, SEL_RATIO).sum(-1)
    blk = jnp.arange(n_sblk)[None, :]
    cur = (t_pos // SEL_BLOCK)[:, None]
    valid_b = blk * SEL_BLOCK <= t_pos[:, None]
    forced = (blk == 0) | (blk == cur) | (blk == cur - 1)
    score = jnp.where(forced[:, None, :], BIG, jnp.where(valid_b[:, None, :], imp, NEG))
    _, sel = lax.top_k(score, n_sel)
    ks, vs = gather_sel(sel)
    n_keys = n_sel * SEL_BLOCK
    ks = ks.reshape(B, Tq, NSA_KV, n_keys, HEAD_DIM)
    vs = vs.reshape(B, Tq, NSA_KV, n_keys, HEAD_DIM)
    s_pos = (sel[..., None] * SEL_BLOCK + jnp.arange(SEL_BLOCK)).reshape(B, Tq, NSA_KV, n_keys)
    dist_s = t_pos[None, :, None, None] - s_pos
    bias_s = bt[t5_bucket(dist_s), jnp.arange(NSA_KV)[:, None]]
    ls = jnp.einsum('bqgrd,bqgnd->bqgrn', q, ks).astype(F32) * scale + jnp.swapaxes(bias_s, -1, -2)
    ps = jax.nn.softmax(jnp.where((dist_s >= 0)[:, :, :, None, :], ls, NEG), axis=-1)
    o_s = jnp.einsum('bqgrn,bqgnd->bqgrd', ps.astype(vs.dtype), vs)
    dist_w = t_pos[:, None] - w_pos[None, :]
    valid_w = (dist_w >= 0) & (dist_w < WINDOW) & (w_pos >= 0)[None, :]
    lw = jnp.einsum('bqgrd,bngd->bgrqn', q, wk).astype(F32) * scale + bt[t5_bucket(dist_w)].transpose(2, 3, 0, 1)
    pw = jax.nn.softmax(jnp.where(valid_w, lw, NEG), axis=-1)
    o_w = jnp.einsum('bgrqn,bngd->bqgrd', pw.astype(wv.dtype), wv)
    return gates[..., 0:1] * o_c + gates[..., 1:2] * o_s + gates[..., 2:3] * o_w


def nsa_prompt(m, w_phi, g_ck, bias_tab):
    q, gates, skv, wkv = m['nsa_q'], m['nsa_gate'], m['sel_kv'], m['win_kv']
    B, S = q.shape[:2]
    ck, cv = compress(m['cmp_kv'], w_phi, g_ck)
    n_sblk = S // SEL_BLOCK
    n_sel = min(N_SEL, n_sblk)
    wpad = jnp.pad(wkv, ((0, 0), (WINDOW, 0), (0, 0), (0, 0), (0, 0)))
    b_idx = jnp.arange(B)[:, None, None, None, None, None]
    c_idx = jnp.arange(2)
    g_idx = jnp.arange(NSA_KV)[:, None, None, None]

    def gather_sel(sel):
        rows = sel[..., None] * SEL_BLOCK + jnp.arange(SEL_BLOCK)
        g = skv[b_idx, rows[..., None], c_idx, g_idx]
        return g[..., 0, :], g[..., 1, :]

    def block(i):
        st = i * Q_BLOCK
        t_pos = st + jnp.arange(Q_BLOCK)
        wb = lax.dynamic_slice_in_dim(wpad, st, WINDOW + Q_BLOCK, axis=1)
        w_pos = st - WINDOW + jnp.arange(WINDOW + Q_BLOCK)
        qb = lax.dynamic_slice_in_dim(q, st, Q_BLOCK, axis=1)
        gb = lax.dynamic_slice_in_dim(gates, st, Q_BLOCK, axis=1)
        return nsa_core(qb, gb, t_pos, ck, cv, gather_sel, n_sblk, n_sel, wb[:, :, 0], wb[:, :, 1], w_pos, bias_tab)

    o = lax.map(block, jnp.arange(S // Q_BLOCK))
    return jnp.moveaxis(o, 0, 1).reshape(B, S, NSA_HEADS * HEAD_DIM)


def nsa_sample(m, cache_cmp, cache_sel, state_win, l, page_table, w_phi, g_ck, bias_tab):
    q, gates, ckv_new, skv_new, wkv_new = m['nsa_q'], m['nsa_gate'], m['cmp_kv'], m['sel_kv'], m['win_kv']
    DB, DS = q.shape[:2]
    past = page_table.shape[1] * PAGE_SIZE
    L = past + DS
    t_pos = past + jnp.arange(DS)
    cmp_past = cache_cmp[l, page_table].reshape(DB, past, 2, NSA_KV, HEAD_DIM)
    ck, cv = compress(cmp_past, w_phi, g_ck)
    if DS >= CMP_BLOCK:
        ck_n, cv_n = compress(ckv_new, w_phi, g_ck)
        ck = jnp.concatenate([ck, ck_n], axis=1)
        cv = jnp.concatenate([cv, cv_n], axis=1)
    n_sblk = -(-L // SEL_BLOCK)
    n_sel = min(N_SEL, n_sblk)
    n_new_rows = n_sblk * SEL_BLOCK - past
    skv_pad = jnp.pad(skv_new, ((0, 0), (0, n_new_rows - DS), (0, 0), (0, 0), (0, 0)))
    b5 = jnp.arange(DB)[:, None, None, None, None]
    b6 = b5[..., None]
    c_idx = jnp.arange(2)
    g_idx = jnp.arange(NSA_KV)[:, None, None, None]

    def gather_sel(sel):
        rows = sel[..., None] * SEL_BLOCK + jnp.arange(SEL_BLOCK)
        rp = jnp.minimum(rows, past - 1)
        page = page_table[b5, rp // PAGE_SIZE]
        g_pool = cache_sel[l, page[..., None], (rp % PAGE_SIZE)[..., None], c_idx, g_idx]
        g_new = skv_pad[b6, jnp.clip(rows - past, 0, n_new_rows - 1)[..., None], c_idx, g_idx]
        g = jnp.where((rows < past)[..., None, None], g_pool, g_new)
        return g[..., 0, :], g[..., 1, :]

    wb_len = state_win.shape[2]
    win = jnp.concatenate([state_win[l], wkv_new], axis=1)
    w_pos = past - wb_len + jnp.arange(wb_len + DS)
    o = nsa_core(q, gates, t_pos, ck, cv, gather_sel, n_sblk, n_sel, win[:, :, 0], win[:, :, 1], w_pos, bias_tab)
    new_win = win[:, win.shape[1] - min(WINDOW, L):]
    return o.reshape(DB, DS, NSA_HEADS * HEAD_DIM), new_win


def merge_out(h, m, o_a, o_b, w_a, w_b, w_o):
    merged = m['gate_a'] * (o_a @ w_a) + m['gate_b'] * (o_b @ w_b)
    return h + merged @ w_o


def setup_inputs(seed: int = 0) -> dict:
    key = jax.random.key(seed)
    ks = jax.random.split(key, 32)
    n_pages = PAST_LEN // PAGE_SIZE
    n_used = DEC_BATCH * n_pages
    n_pool = n_used + max(1, n_used // 4)
    win_buf = min(WINDOW, PAST_LEN)

    def nrm(k, shape, scale=1.0):
        return jax.random.normal(k, shape, F32) * scale

    def gain(k, shape):
        return 1.0 + 0.02 * jax.random.normal(k, shape, F32)

    page_table = jax.random.permutation(ks[0], n_pool)[:n_used].reshape(DEC_BATCH, n_pages).astype(jnp.int32)
    return {
        'x_prompt': nrm(ks[1], (BATCH, SEQ, D_MODEL)),
        'x_sample': nrm(ks[2], (DEC_BATCH, DEC_SEQ, D_MODEL)),
        'cache_dsa_kv': nrm(ks[3], (DEPTH, n_pool, PAGE_SIZE, 2, DSA_KV, HEAD_DIM)),
        'cache_dsa_idx_k': nrm(ks[4], (DEPTH, n_pool, PAGE_SIZE, IDX_DIM)),
        'cache_nsa_cmp_kv': nrm(ks[5], (DEPTH, n_pool, PAGE_SIZE, 2, NSA_KV, HEAD_DIM)),
        'cache_nsa_sel_kv': nrm(ks[6], (DEPTH, n_pool, PAGE_SIZE, 2, NSA_KV, HEAD_DIM)),
        'state_nsa_win_kv': nrm(ks[7], (DEPTH, DEC_BATCH, win_buf, 2, NSA_KV, HEAD_DIM)),
        'page_table': page_table,
        'rel_bias': nrm(ks[8], (NUM_BUCKETS, N_HEADS), 0.5),
        'ffn1_norm': gain(ks[9], (DEPTH, D_MODEL)),
        'ffn1_w_gu': nrm(ks[10], (DEPTH, D_MODEL, 2 * D_FF), D_MODEL ** -0.5),
        'ffn1_w_down': nrm(ks[11], (DEPTH, D_FF, D_MODEL), D_FF ** -0.5),
        'mix_norm': gain(ks[12], (DEPTH, D_MODEL)),
        'w_in': nrm(ks[13], (DEPTH, D_MODEL, D_IN), D_MODEL ** -0.5),
        'dsa_q_norm': gain(ks[14], (DEPTH, HEAD_DIM)),
        'dsa_k_norm': gain(ks[15], (DEPTH, HEAD_DIM)),
        'idx_k_norm': gain(ks[16], (DEPTH, IDX_DIM)),
        'nsa_q_norm': gain(ks[17], (DEPTH, HEAD_DIM)),
        'nsa_cmp_k_norm': gain(ks[18], (DEPTH, HEAD_DIM)),
        'nsa_sel_k_norm': gain(ks[19], (DEPTH, HEAD_DIM)),
        'nsa_win_k_norm': gain(ks[20], (DEPTH, HEAD_DIM)),
        'nsa_w_phi': nrm(ks[21], (DEPTH, 2, CMP_BLOCK, HEAD_DIM, HEAD_DIM), (CMP_BLOCK * HEAD_DIM) ** -0.5),
        'w_dsa_o': nrm(ks[22], (DEPTH, DSA_HEADS * HEAD_DIM, D_MODEL), (DSA_HEADS * HEAD_DIM) ** -0.5),
        'w_nsa_o': nrm(ks[23], (DEPTH, NSA_HEADS * HEAD_DIM, D_MODEL), (NSA_HEADS * HEAD_DIM) ** -0.5),
        'w_out': nrm(ks[24], (DEPTH, D_MODEL, D_MODEL), D_MODEL ** -0.5),
        'ffn2_norm': gain(ks[25], (DEPTH, D_MODEL)),
        'ffn2_w_gu': nrm(ks[26], (DEPTH, D_MODEL, 2 * D_FF), D_MODEL ** -0.5),
        'ffn2_w_down': nrm(ks[27], (DEPTH, D_FF, D_MODEL), D_FF ** -0.5),
    }


def reference(x_prompt, x_sample, cache_dsa_kv, cache_dsa_idx_k, cache_nsa_cmp_kv, cache_nsa_sel_kv,
              state_nsa_win_kv, page_table, rel_bias, ffn1_norm, ffn1_w_gu, ffn1_w_down, mix_norm, w_in,
              dsa_q_norm, dsa_k_norm, idx_k_norm, nsa_q_norm, nsa_cmp_k_norm, nsa_sel_k_norm, nsa_win_k_norm,
              nsa_w_phi, w_dsa_o, w_nsa_o, w_out, ffn2_norm, ffn2_w_gu, ffn2_w_down):
    dsa_bias = rel_bias[:, :DSA_HEADS]
    nsa_bias = rel_bias[:, DSA_HEADS:]
    hp, hs = x_prompt, x_sample
    p_dkv, p_ik, p_ckv, p_skv, p_wkv = [], [], [], [], []
    s_dkv, s_ik, s_ckv, s_skv, s_wkv = [], [], [], [], []
    for l in range(DEPTH):
        hp = hp + 0.5 * swiglu(hp, ffn1_norm[l], ffn1_w_gu[l], ffn1_w_down[l])
        mp = mixer_inputs(hp, mix_norm[l], w_in[l], dsa_q_norm[l], dsa_k_norm[l], idx_k_norm[l],
                          nsa_q_norm[l], nsa_sel_k_norm[l], nsa_win_k_norm[l])
        oa = dsa_prompt(mp, dsa_bias)
        ob = nsa_prompt(mp, nsa_w_phi[l], nsa_cmp_k_norm[l], nsa_bias)
        hp = merge_out(hp, mp, oa, ob, w_dsa_o[l], w_nsa_o[l], w_out[l])
        hp = hp + 0.5 * swiglu(hp, ffn2_norm[l], ffn2_w_gu[l], ffn2_w_down[l])
        p_dkv.append(mp['dsa_kv'])
        p_ik.append(mp['idx_k'])
        p_ckv.append(mp['cmp_kv'])
        p_skv.append(mp['sel_kv'])
        p_wkv.append(mp['win_kv'][:, mp['win_kv'].shape[1] - min(WINDOW, mp['win_kv'].shape[1]):])
        hs = hs + 0.5 * swiglu(hs, ffn1_norm[l], ffn1_w_gu[l], ffn1_w_down[l])
        ms = mixer_inputs(hs, mix_norm[l], w_in[l], dsa_q_norm[l], dsa_k_norm[l], idx_k_norm[l],
                          nsa_q_norm[l], nsa_sel_k_norm[l], nsa_win_k_norm[l])
        oa_s = dsa_sample(ms, cache_dsa_kv, cache_dsa_idx_k, l, page_table, dsa_bias)
        ob_s, win_s = nsa_sample(ms, cache_nsa_cmp_kv, cache_nsa_sel_kv, state_nsa_win_kv, l, page_table,
                                 nsa_w_phi[l], nsa_cmp_k_norm[l], nsa_bias)
        hs = merge_out(hs, ms, oa_s, ob_s, w_dsa_o[l], w_nsa_o[l], w_out[l])
        hs = hs + 0.5 * swiglu(hs, ffn2_norm[l], ffn2_w_gu[l], ffn2_w_down[l])
        s_dkv.append(ms['dsa_kv'])
        s_ik.append(ms['idx_k'])
        s_ckv.append(ms['cmp_kv'])
        s_skv.append(ms['sel_kv'])
        s_wkv.append(win_s)
    return (hp, hs,
            jnp.stack(p_dkv), jnp.stack(p_ik), jnp.stack(p_ckv), jnp.stack(p_skv), jnp.stack(p_wkv),
            jnp.stack(s_dkv), jnp.stack(s_ik), jnp.stack(s_ckv), jnp.stack(s_skv), jnp.stack(s_wkv))
```

```python
import functools
import math

import numpy as np
import jax
import jax.numpy as jnp
from jax import lax
from jax.experimental import pallas as pl
from jax.experimental.pallas import tpu as pltpu

F32 = jnp.float32
BF16 = jnp.bfloat16
I32 = jnp.int32

HEAD_DIM = 64
LANES = 128
N_GROUPS = 2
GROUP_HEADS = 4
MIX_HEADS = N_GROUPS * GROUP_HEADS
IDX_HEADS = 8
IDX_TOPK = 256
CMP_BLOCK = 32
SEL_BLOCK = 64
N_SEL = 16
WINDOW = 512
NUM_BUCKETS = 32
MAX_EXACT = 16
MAX_DISTANCE = 128
PAGE = 128
EPS = 1e-6
NEG = -1e30
BIG = 1e30
TQ = 256
DS_PAD = 8
VMEM_LIMIT = 56 * 1024 * 1024

INT_MIN = -(2 ** 31)
NEG_KEY = int(np.array([NEG], np.float32).view(np.int32)[0]) ^ 0x7FFFFFFF


def _bucket_thresholds():
    n = np.arange(MAX_EXACT, 4 * MAX_DISTANCE)
    ratio = np.log(n.astype(np.float32) / np.float32(MAX_EXACT)) / np.float32(math.log(MAX_DISTANCE / MAX_EXACT))
    big = np.minimum(MAX_EXACT + (ratio * np.float32(NUM_BUCKETS - MAX_EXACT)).astype(np.int32), NUM_BUCKETS - 1)
    return [int(n[np.argmax(big >= b)]) for b in range(MAX_EXACT + 1, NUM_BUCKETS)]


BUCKET_THR = _bucket_thresholds()
FAR_DIST = BUCKET_THR[-1]


def _dot(a, b):
    return jnp.dot(a, b, preferred_element_type=F32)


def _dot_nt(a, b):
    return lax.dot_general(a, b, (((1,), (1,)), ((), ())), preferred_element_type=F32)


def _cparams(*sem):
    return pltpu.CompilerParams(dimension_semantics=sem, vmem_limit_bytes=VMEM_LIMIT)


def _full(shape):
    nd = len(shape)
    return pl.BlockSpec(shape, lambda *_: (0,) * nd)


def _rms_rows(x, g):
    return x * lax.rsqrt(jnp.mean(x * x, axis=-1, keepdims=True) + EPS) * g


def _norm64(z, gain, s64):
    outs = []
    for j in range(z.shape[1] // LANES):
        zb = z[:, j * LANES:(j + 1) * LANES]
        sq = zb * zb
        hi = sq.astype(BF16)
        lo = (sq - hi.astype(F32)).astype(BF16)
        ms = (_dot(hi, s64) + _dot(lo, s64)) * (1.0 / HEAD_DIM)
        outs.append(zb * lax.rsqrt(ms + EPS) * gain[:, j * LANES:(j + 1) * LANES])
    return outs[0] if len(outs) == 1 else jnp.concatenate(outs, axis=1)


def _ffn_kernel(x_ref, g_ref, wg_ref, wu_ref, wd_ref, o_ref, xn_ref, acc_ref):
    x = x_ref[...]
    xn_ref[...] = _rms_rows(x, g_ref[...]).astype(BF16)
    acc_ref[...] = jnp.zeros_like(acc_ref)

    def body(c, carry):
        xn = xn_ref[...]
        a = _dot(xn, wg_ref[c])
        b = _dot(xn, wu_ref[c])
        act = (a * jax.nn.sigmoid(a) * b).astype(BF16)
        acc_ref[...] += _dot(act, wd_ref[c])
        return carry

    lax.fori_loop(0, wg_ref.shape[0], body, 0)
    o_ref[...] = x + 0.5 * acc_ref[...]


def _ffn(x, g, w_gu, w_dn, tm):
    n, d = x.shape
    d_ff = w_dn.shape[0]
    fc = 256
    nc = d_ff // fc
    assert nc * fc == d_ff and n % tm == 0
    wg = w_gu[:, :d_ff].reshape(d, nc, fc).transpose(1, 0, 2).astype(BF16)
    wu = w_gu[:, d_ff:].reshape(d, nc, fc).transpose(1, 0, 2).astype(BF16)
    wd = w_dn.reshape(nc, fc, d).astype(BF16)
    return pl.pallas_call(
        _ffn_kernel,
        grid=(n // tm,),
        in_specs=[pl.BlockSpec((tm, d), lambda i: (i, 0)), _full((1, d)),
                  _full((nc, d, fc)), _full((nc, d, fc)), _full((nc, fc, d))],
        out_specs=pl.BlockSpec((tm, d), lambda i: (i, 0)),
        out_shape=jax.ShapeDtypeStruct((n, d), F32),
        scratch_shapes=[pltpu.VMEM((tm, d), BF16), pltpu.VMEM((tm, d), F32)],
        compiler_params=_cparams("arbitrary"),
        name="ffn",
    )(x, g.reshape(1, d), wg, wu, wd)


_W_QD, _W_IQ, _W_QN = 0, 1024, 2048
_W_DKV, _W_CKV, _W_SKV, _W_WKV = 3072, 3328, 3584, 3840
_W_MG, _W_IK, _W_MISC, _W_TOTAL = 4096, 6144, 6272, 6400
MISC_IW = 0
MISC_NG = 8


def _proj_kernel(h_ref, gm_ref, w_ref, gqd_ref, gqn_ref, gk_ref, gik_ref, s64_ref,
                 qd_ref, iq_ref, qn_ref, dkv_ref, dk_ref, dv_ref, ckv_ref, skv_ref, sk_ref, sv_ref,
                 wkv_ref, wk_ref, wv_ref, ga_ref, gb_ref, ik_ref, ika_ref, iw_ref, ng_ref, hn_ref):
    hn_ref[...] = _rms_rows(h_ref[...], gm_ref[...]).astype(BF16)
    s64 = s64_ref[...]

    def seg(a, b):
        return _dot(hn_ref[...], w_ref[:, a:b])

    qd_ref[...] = _norm64(seg(_W_QD, _W_IQ), gqd_ref[...], s64).astype(qd_ref.dtype)
    iq_ref[...] = (seg(_W_IQ, _W_QN) * (HEAD_DIM ** -0.5)).astype(iq_ref.dtype)
    qn_ref[...] = _norm64(seg(_W_QN, _W_DKV), gqn_ref[...], s64).astype(qn_ref.dtype)
    for j, (a, full_ref, k_ref, v_ref) in enumerate(((_W_DKV, dkv_ref, dk_ref, dv_ref),
                                                     (_W_SKV, skv_ref, sk_ref, sv_ref),
                                                     (_W_WKV, wkv_ref, wk_ref, wv_ref))):
        z = seg(a, a + 2 * LANES)
        kn = _norm64(z[:, :LANES], gk_ref[:, j * LANES:(j + 1) * LANES], s64)
        v = z[:, LANES:]
        full_ref[:, :LANES] = kn
        full_ref[:, LANES:] = v
        k_ref[...] = kn.astype(BF16)
        v_ref[...] = v.astype(BF16)
    ckv_ref[...] = seg(_W_CKV, _W_SKV)
    half = (_W_IK - _W_MG) // 2
    ga_ref[...] = jax.nn.sigmoid(seg(_W_MG, _W_MG + half))
    gb_ref[...] = jax.nn.sigmoid(seg(_W_MG + half, _W_IK))
    ikn = _norm64(seg(_W_IK, _W_MISC), gik_ref[...], s64)
    ik_ref[...] = ikn[:, :HEAD_DIM]
    ika_ref[...] = ikn.astype(BF16)
    misc = seg(_W_MISC, _W_TOTAL)
    iw_ref[...] = misc * (IDX_HEADS ** -0.5)
    ng_ref[...] = jax.nn.sigmoid(misc)


def _pad_heads(w, by_group):
    d = w.shape[0]
    w = w.reshape(d, MIX_HEADS, HEAD_DIM)
    z = jnp.zeros_like(w)
    if by_group:
        grp = (jnp.arange(MIX_HEADS) // GROUP_HEADS)[None, :, None]
        lo, hi = jnp.where(grp == 0, w, z), jnp.where(grp == 1, w, z)
    else:
        lo, hi = w, z
    return jnp.concatenate([lo, hi], axis=-1).reshape(d, MIX_HEADS * LANES)


def _proj_weights(w_in, dsa_q_norm, dsa_k_norm, idx_k_norm, nsa_q_norm, nsa_sel_k_norm, nsa_win_k_norm):
    d = w_in.shape[0]
    sizes = (512, 256, 512, 8, 64, 512, 256, 256, 256, 24, 2 * d)
    offs = np.concatenate([[0], np.cumsum(sizes)])
    dq, dkv, iq, iw, ik, nq, ckv, skv, wkv, ng, mg = [w_in[:, offs[j]:offs[j + 1]] for j in range(len(sizes))]
    misc = jnp.concatenate([iw, ng, jnp.zeros((d, LANES - 32), F32)], axis=1)
    ikp = jnp.concatenate([ik, jnp.zeros((d, LANES - HEAD_DIM), F32)], axis=1)
    w = jnp.concatenate([_pad_heads(dq, True), _pad_heads(iq, False), _pad_heads(nq, True),
                         dkv, ckv, skv, wkv, mg, ikp, misc], axis=1).astype(BF16)
    assert w.shape[1] == _W_TOTAL
    scale = HEAD_DIM ** -0.5
    gqd = _pad_heads(jnp.tile(dsa_q_norm * scale, MIX_HEADS)[None, :], True)
    gqn = _pad_heads(jnp.tile(nsa_q_norm * scale, MIX_HEADS)[None, :], True)
    gk = jnp.concatenate([jnp.tile(g, 2) for g in (dsa_k_norm, nsa_sel_k_norm, nsa_win_k_norm)])[None, :]
    gik = jnp.concatenate([idx_k_norm, jnp.zeros((LANES - HEAD_DIM,), F32)])[None, :]
    return w, gqd, gqn, gk, gik


def _seg_matrix():
    r = np.arange(LANES) // HEAD_DIM
    return jnp.asarray((r[:, None] == r[None, :]).astype(np.float32), BF16)


def _proj(h, g_mix, pw, tm, q_dtype):
    n, d = h.shape
    w, gqd, gqn, gk, gik = pw
    row = lambda width: pl.BlockSpec((tm, width), lambda i: (i, 0))
    outs = [("qd", 1024, q_dtype), ("iq", 1024, q_dtype), ("qn", 1024, q_dtype),
            ("dkv", 256, F32), ("dk", 128, BF16), ("dv", 128, BF16), ("ckv", 256, F32),
            ("skv", 256, F32), ("sk", 128, BF16), ("sv", 128, BF16),
            ("wkv", 256, F32), ("wk", 128, BF16), ("wv", 128, BF16),
            ("ga", d, F32), ("gb", d, F32), ("ik", HEAD_DIM, F32), ("ika", 128, BF16),
            ("iw", 128, F32), ("ng", 128, F32)]
    res = pl.pallas_call(
        _proj_kernel,
        grid=(n // tm,),
        in_specs=[row(d), _full((1, d)), _full(w.shape), _full(gqd.shape), _full(gqn.shape),
                  _full(gk.shape), _full(gik.shape), _full((LANES, LANES))],
        out_specs=[row(wd) for _, wd, _ in outs],
        out_shape=[jax.ShapeDtypeStruct((n, wd), dt) for _, wd, dt in outs],
        scratch_shapes=[pltpu.VMEM((tm, d), BF16)],
        compiler_params=_cparams("arbitrary"),
        name="proj",
    )(h, g_mix.reshape(1, d), w, gqd, gqn, gk, gik, _seg_matrix())
    return {name: r for (name, _, _), r in zip(outs, res)}


def _bias_tile(dist, valid, tab_ref, h):
    n = jnp.maximum(dist, 0)
    big = jnp.full(n.shape, MAX_EXACT, I32)
    for thr in BUCKET_THR:
        big = big + (n >= thr).astype(I32)
    bucket = jnp.where(n < MAX_EXACT, n, big)
    far = tab_ref[NUM_BUCKETS - 1, h]
    out = jnp.zeros(n.shape, F32)
    for b in range(NUM_BUCKETS - 1):
        out = jnp.where(bucket == b, tab_ref[b, h] - far, out)
    return jnp.where(valid, out, NEG)


def _iota2(shape, axis):
    return lax.broadcasted_iota(I32, shape, axis)


def _tiles_kernel(tab_ref, diag_ref, prev_ref, slast_ref, snew_ref, swin_ref, scmp_ref, *, n_cmp):
    h = pl.program_id(0)
    r = _iota2((TQ, TQ), 0)
    c = _iota2((TQ, TQ), 1)
    diag_ref[0] = _bias_tile(r - c, r >= c, tab_ref, h)
    prev_ref[0] = _bias_tile(TQ + r - c, r >= -1, tab_ref, h)
    r8 = _iota2((DS_PAD, PAGE), 0)
    c8 = _iota2((DS_PAD, PAGE), 1)
    slast_ref[0] = _bias_tile(PAGE + r8 - c8, r8 >= -1, tab_ref, h)
    snew_ref[0] = _bias_tile(r8 - c8, r8 >= c8, tab_ref, h)
    rw = _iota2((DS_PAD, WINDOW), 0)
    cw = _iota2((DS_PAD, WINDOW), 1)
    swin_ref[0] = _bias_tile(WINDOW + rw - cw, cw > rw, tab_ref, h)
    rc = _iota2((DS_PAD, n_cmp), 0)
    uc = _iota2((DS_PAD, n_cmp), 1)
    half = n_cmp // 2
    jc = 2 * jnp.where(uc >= half, uc - half, uc) + (uc >= half).astype(I32)
    scmp_ref[0] = _bias_tile(n_cmp * CMP_BLOCK + rc - (jc * CMP_BLOCK + CMP_BLOCK - 1), rc >= -1, tab_ref, h)


def _bias_tiles(rel_bias, n_cmp_sample):
    nh = rel_bias.shape[1]
    per_head = lambda *s: pl.BlockSpec((1,) + s, lambda h: (h,) + (0,) * len(s))
    shapes = [(TQ, TQ), (TQ, TQ), (DS_PAD, PAGE), (DS_PAD, PAGE), (DS_PAD, WINDOW), (DS_PAD, n_cmp_sample)]
    return pl.pallas_call(
        functools.partial(_tiles_kernel, n_cmp=n_cmp_sample),
        grid=(nh,),
        in_specs=[pl.BlockSpec(memory_space=pltpu.SMEM)],
        out_specs=[per_head(*s) for s in shapes],
        out_shape=[jax.ShapeDtypeStruct((nh,) + s, F32) for s in shapes],
        compiler_params=_cparams("arbitrary"),
        name="bias_tiles",
    )(rel_bias)


def _cmp_tiles_kernel(tab_ref, o_ref, *, head0, n_cmp):
    h = pl.program_id(0) + head0
    i = pl.program_id(1)
    r = _iota2((TQ, n_cmp), 0)
    u = _iota2((TQ, n_cmp), 1)
    half = n_cmp // 2
    j = 2 * jnp.where(u >= half, u - half, u) + (u >= half).astype(I32)
    dist = i * TQ + r - (j * CMP_BLOCK + CMP_BLOCK - 1)
    o_ref[0, 0] = _bias_tile(dist, dist >= 0, tab_ref, h)


def _cmp_tiles(rel_bias, head0, n_qblk, n_cmp):
    return pl.pallas_call(
        functools.partial(_cmp_tiles_kernel, head0=head0, n_cmp=n_cmp),
        grid=(MIX_HEADS, n_qblk),
        in_specs=[pl.BlockSpec(memory_space=pltpu.SMEM)],
        out_specs=pl.BlockSpec((1, 1, TQ, n_cmp), lambda h, i: (i, h, 0, 0)),
        out_shape=jax.ShapeDtypeStruct((n_qblk, MIX_HEADS, TQ, n_cmp), F32),
        compiler_params=_cparams("arbitrary", "arbitrary"),
        name="cmp_tiles",
    )(rel_bias)


def _sortable(x):
    b = lax.bitcast_convert_type(x, I32)
    return jnp.where(x == 0.0, 0, b ^ ((b >> 31) & 0x7FFFFFFF))


def _kth_largest(count_ge, rows, k):
    zero = jnp.zeros((rows, LANES), I32)
    t0 = jnp.where(count_ge(zero) >= k, zero, jnp.full((rows, LANES), INT_MIN, I32))

    def bit_body(bi, t):
        cand = t + jnp.left_shift(jnp.int32(1), 30 - bi)
        return jnp.where(count_ge(cand) >= k, cand, t)

    return lax.fori_loop(0, 31, bit_body, t0)


def _lane_sum_i32(x):
    s = jnp.sum(x, axis=-1, keepdims=True)
    return jnp.broadcast_to(s, x.shape)


def _fold_lanes(x):
    out = x[:, :LANES]
    for j in range(1, x.shape[1] // LANES):
        out = out + x[:, j * LANES:(j + 1) * LANES]
    return out


def _rep_lanes(x, width):
    n = width // LANES
    return x if n == 1 else jnp.concatenate([x] * n, axis=1)


def _topk_select_bias(keys, thr, need, run_ref, utri):
    width = keys.shape[1]
    thr_w = _rep_lanes(thr, width)
    eq = keys == thr_w
    eq_f = jnp.where(eq, 1.0, 0.0)
    before = _dot(eq_f.astype(BF16), utri) + _rep_lanes(run_ref[...], width)
    tie_ok = before < _rep_lanes(need, width)
    run_ref[...] += jnp.broadcast_to(jnp.sum(eq_f, axis=-1, keepdims=True), run_ref.shape)
    return jnp.where(keys > thr_w, 0.0, jnp.where(eq, jnp.where(tie_ok, 0.0, NEG), NEG))


def _flash_init(m_ref, l_ref, acc_ref):
    m_ref[...] = jnp.full(m_ref.shape, -3e38, F32)
    l_ref[...] = jnp.zeros_like(l_ref)
    acc_ref[...] = jnp.zeros_like(acc_ref)


def _flash_update(lg, v, m_ref, l_ref, acc_ref):
    m_old = m_ref[...]
    m_new = jnp.maximum(m_old, jnp.max(lg, axis=-1, keepdims=True))
    alpha = jnp.exp(m_old - m_new)
    p = jnp.exp(lg - m_new)
    l_ref[...] = alpha * l_ref[...] + jnp.sum(p, axis=-1, keepdims=True)
    acc_ref[...] = alpha * acc_ref[...] + _dot(p.astype(BF16), v)
    m_ref[...] = m_new


def _stack_heads(src, dst_ref, heads, rows):
    for h in range(heads):
        dst_ref[h * rows:(h + 1) * rows, :] = src[:, h * LANES:(h + 1) * LANES].astype(dst_ref.dtype)


def _upper_tri(n):
    r = np.arange(n)
    return jnp.asarray((r[:, None] < r[None, :]).astype(np.float32), BF16)


def _dsa_prompt_kernel(qa_ref, iqa_ref, iw_ref, ika_ref, k_ref, v_ref, tdiag_ref, tprev_ref, utri_ref,
                       o_ref, keys_ref, qs_ref, iqs_ref, wb_ref, run_ref, m_ref, l_ref, acc_ref, *, k_top):
    i = pl.program_id(1)
    nh = MIX_HEADS
    _stack_heads(qa_ref[0], qs_ref, nh, TQ)
    _stack_heads(iqa_ref[0], iqs_ref, IDX_HEADS, TQ)
    iw = iw_ref[0]
    for h in range(IDX_HEADS):
        wb_ref[h] = jnp.broadcast_to(iw[:, MISC_IW + h:MISC_IW + h + 1], (TQ, LANES))

    def chunk(ref, c):
        return ref[0, pl.ds(pl.multiple_of(c * TQ, TQ), TQ), :]

    def score_chunk(c, causal):
        s = _dot_nt(iqs_ref[...], chunk(ika_ref, c))
        sc = jnp.zeros((TQ, TQ), F32)
        for h in range(IDX_HEADS):
            sc = sc + jnp.maximum(s[h * TQ:(h + 1) * TQ], 0.0) * _rep_lanes(wb_ref[h], TQ)
        if causal:
            sc = jnp.where(_iota2((TQ, TQ), 0) >= _iota2((TQ, TQ), 1), sc, NEG)
        keys_ref[c] = _sortable(sc)

    def score_body(c, carry):
        score_chunk(c, False)
        return carry

    lax.fori_loop(0, i, score_body, 0)
    score_chunk(i, True)

    def count_ge(cand):
        cw = _rep_lanes(cand, TQ)

        def body(c, acc):
            return acc + _fold_lanes((keys_ref[c] >= cw).astype(I32))

        return _lane_sum_i32(lax.fori_loop(0, i + 1, body, jnp.zeros((TQ, LANES), I32)))

    thr = _kth_largest(count_ge, TQ, k_top)
    need = (k_top - count_ge(thr + 1)).astype(F32)

    _flash_init(m_ref, l_ref, acc_ref)
    run_ref[...] = jnp.zeros_like(run_ref)

    def attend(c, tile_ref):
        sel = _topk_select_bias(keys_ref[c], thr, need, run_ref, utri_ref[...])
        lg = _dot_nt(qs_ref[...], chunk(k_ref, c))
        parts = []
        for h in range(nh):
            part = lg[h * TQ:(h + 1) * TQ] + sel
            if tile_ref is not None:
                part = part + tile_ref[h]
            parts.append(part)
        _flash_update(jnp.concatenate(parts, axis=0), chunk(v_ref, c), m_ref, l_ref, acc_ref)

    def far_body(c, carry):
        attend(c, None)
        return carry

    lax.fori_loop(0, jnp.maximum(i - 1, 0), far_body, 0)

    @pl.when(i >= 1)
    def _():
        attend(i - 1, tprev_ref)

    attend(i, tdiag_ref)
    inv = 1.0 / l_ref[...]
    for h in range(nh):
        rows = slice(h * TQ, (h + 1) * TQ)
        o_ref[0, :, h * LANES:(h + 1) * LANES] = (acc_ref[rows, :] * inv[rows]).astype(o_ref.dtype)


def _dsa_prompt(m, tdiag, tprev, b, s):
    nq = s // TQ
    k_top = min(IDX_TOPK, s // 4)
    qblk = lambda width: pl.BlockSpec((1, TQ, width), lambda bi, i: (bi, i, 0))
    seq = lambda width: pl.BlockSpec((1, s, width), lambda bi, i: (bi, 0, 0))
    r3 = lambda a: a.reshape(b, s, a.shape[-1])
    return pl.pallas_call(
        functools.partial(_dsa_prompt_kernel, k_top=k_top),
        grid=(b, nq),
        in_specs=[qblk(1024), qblk(1024), qblk(LANES), seq(LANES), seq(LANES), seq(LANES),
                  _full((MIX_HEADS, TQ, TQ)), _full((MIX_HEADS, TQ, TQ)), _full((TQ, TQ))],
        out_specs=qblk(1024),
        out_shape=jax.ShapeDtypeStruct((b, s, MIX_HEADS * LANES), BF16),
        scratch_shapes=[pltpu.VMEM((nq, TQ, TQ), I32),
                        pltpu.VMEM((MIX_HEADS * TQ, LANES), BF16),
                        pltpu.VMEM((IDX_HEADS * TQ, LANES), BF16),
                        pltpu.VMEM((IDX_HEADS, TQ, LANES), F32),
                        pltpu.VMEM((TQ, LANES), F32),
                        pltpu.VMEM((MIX_HEADS * TQ, 1), F32),
                        pltpu.VMEM((MIX_HEADS * TQ, 1), F32),
                        pltpu.VMEM((MIX_HEADS * TQ, LANES), F32)],
        compiler_params=_cparams("arbitrary", "arbitrary"),
        name="dsa_prompt",
    )(r3(m["qd"]), r3(m["iq"]), r3(m["iw"]), r3(m["ika"]), r3(m["dk"]), r3(m["dv"]),
      tdiag, tprev, _upper_tri(TQ))


def _compress_math(x_even, x_odd, wbig_ref, gck, s64, ck_ref, cv_ref):
    half = x_even.shape[0]
    for p, xp in enumerate((x_even, x_odd)):
        c = _dot(xp.astype(BF16), wbig_ref[...])
        ck_ref[0, p * half:(p + 1) * half, :] = _norm64(c[:, :LANES], gck, s64).astype(BF16)
        cv_ref[0, p * half:(p + 1) * half, :] = c[:, LANES:].astype(BF16)


def _compress_kernel(x_ref, wbig_ref, gck_ref, s64_ref, ck_ref, cv_ref):
    kdim = wbig_ref.shape[0]
    _compress_math(x_ref[0, :, :kdim], x_ref[0, :, kdim:], wbig_ref, gck_ref[...], s64_ref[...], ck_ref, cv_ref)


def _compress_weights(w_phi, g_ck):
    eye = jnp.eye(N_GROUPS, dtype=F32)
    wbig = jnp.einsum("clde,cx,gy->lcgdxye", w_phi, jnp.eye(2, dtype=F32), eye)
    wbig = wbig.reshape(CMP_BLOCK * 4 * HEAD_DIM, 4 * HEAD_DIM).astype(BF16)
    return wbig, jnp.tile(g_ck, 2)[None, :]


def _compress_prompt(ckv, cw, b, s):
    wbig, gck = cw
    nc = s // CMP_BLOCK
    kdim = wbig.shape[0]
    x = ckv.reshape(b, nc // 2, 2 * kdim)
    out = jax.ShapeDtypeStruct((b, nc, LANES), BF16)
    return pl.pallas_call(
        _compress_kernel,
        grid=(b,),
        in_specs=[pl.BlockSpec((1, nc // 2, 2 * kdim), lambda bi: (bi, 0, 0)), _full(wbig.shape),
                  _full((1, LANES)), _full((LANES, LANES))],
        out_specs=[pl.BlockSpec((1, nc, LANES), lambda bi: (bi, 0, 0))] * 2,
        out_shape=[out, out],
        compiler_params=_cparams("arbitrary"),
        name="compress_prompt",
    )(x, wbig, gck, _seg_matrix())


def _pick_blocks(score, n_sel):
    lane = _iota2(score.shape, 1)
    picked = jnp.zeros(score.shape, F32)
    for _ in range(n_sel):
        top = jnp.max(score, axis=-1, keepdims=True)
        first = jnp.min(jnp.where(score == top, lane, score.shape[1]), axis=-1, keepdims=True)
        hit = lane == first
        picked = jnp.where(hit, 1.0, picked)
        score = jnp.where(hit, -jnp.inf, score)
    return picked


def _softmax_valid(lg):
    valid = lg > 0.5 * NEG
    p = jnp.where(valid, jnp.exp(lg - jnp.max(lg, axis=-1, keepdims=True)), 0.0)
    tot = jnp.sum(p, axis=-1, keepdims=True)
    return p * (1.0 / jnp.where(tot > 0.0, tot, 1.0))


def _nsa_prompt_kernel(qa_ref, ng_ref, ck_ref, cv_ref, ks_ref, vs_ref, kw_ref, vw_ref,
                       tcmp_ref, tdiag_ref, tprev_ref, tw2_ref, exp_ref,
                       o_ref, qs_ref, m_ref, l_ref, acc_ref, *, n_sel):
    i = pl.program_id(1)
    gh = GROUP_HEADS
    rows = gh * TQ
    ng = ng_ref[0]
    n_cmp = ck_ref.shape[1]
    n_blk = n_cmp // 2
    t_row = i * TQ + _iota2((TQ, n_blk), 0)
    blk = _iota2((TQ, n_blk), 1)
    cur = t_row // SEL_BLOCK
    forced = (blk == 0) | (blk == cur) | (blk == cur - 1)

    def chunk(ref, c):
        return ref[0, pl.ds(pl.multiple_of(c * TQ, TQ), TQ), :]

    def tiled(lg, tile_ref, g, extra=None):
        parts = []
        for r in range(gh):
            part = lg[r * TQ:(r + 1) * TQ]
            if tile_ref is not None:
                part = part + tile_ref[g * gh + r]
            if extra is not None:
                part = part + extra
            parts.append(part)
        return jnp.concatenate(parts, axis=0)

    for g in range(N_GROUPS):
        _stack_heads(qa_ref[0, :, g * gh * LANES:(g + 1) * gh * LANES], qs_ref, gh, TQ)
        lc = _dot_nt(qs_ref[...], ck_ref[0])
        lc = jnp.concatenate([lc[r * TQ:(r + 1) * TQ] + tcmp_ref[0, g * gh + r] for r in range(gh)], axis=0)
        pc = _softmax_valid(lc)
        o_c = _dot(pc.astype(BF16), cv_ref[0])
        imp = pc[:TQ]
        for r in range(1, gh):
            imp = imp + pc[r * TQ:(r + 1) * TQ]
        imp = imp[:, :n_blk] + imp[:, n_blk:]
        score = jnp.where(forced, BIG, jnp.where(blk <= cur, imp, NEG))
        picked = _pick_blocks(score, n_sel).astype(BF16)

        _flash_init(m_ref, l_ref, acc_ref)

        def sel_chunk(c, tile_ref):
            keep = _dot(picked, exp_ref[c])
            lg = _dot_nt(qs_ref[...], chunk(ks_ref, c))
            lg = tiled(lg, tile_ref, g, jnp.where(keep > 0.5, 0.0, NEG))
            _flash_update(lg, chunk(vs_ref, c), m_ref, l_ref, acc_ref)

        def far_body(c, carry):
            sel_chunk(c, None)
            return carry

        lax.fori_loop(0, jnp.maximum(i - 1, 0), far_body, 0)

        @pl.when(i >= 1)
        def _():
            sel_chunk(i - 1, tprev_ref)

        sel_chunk(i, tdiag_ref)
        o_s = acc_ref[...] * (1.0 / l_ref[...])

        _flash_init(m_ref, l_ref, acc_ref)

        @pl.when(i >= 2)
        def _():
            lg = _dot_nt(qs_ref[...], chunk(kw_ref, i - 2))
            _flash_update(tiled(lg, None, g, tw2_ref[...]), chunk(vw_ref, i - 2), m_ref, l_ref, acc_ref)

        @pl.when(i >= 1)
        def _():
            lg = _dot_nt(qs_ref[...], chunk(kw_ref, i - 1))
            _flash_update(tiled(lg, tprev_ref, g), chunk(vw_ref, i - 1), m_ref, l_ref, acc_ref)

        lg = _dot_nt(qs_ref[...], chunk(kw_ref, i))
        _flash_update(tiled(lg, tdiag_ref, g), chunk(vw_ref, i), m_ref, l_ref, acc_ref)
        o_w = acc_ref[...] * (1.0 / l_ref[...])

        for r in range(gh):
            h = g * gh + r
            sl = slice(r * TQ, (r + 1) * TQ)
            gate = lambda j: ng[:, MISC_NG + 3 * h + j:MISC_NG + 3 * h + j + 1]
            o = gate(0) * o_c[sl] + gate(1) * o_s[sl] + gate(2) * o_w[sl]
            o_ref[0, :, h * LANES:(h + 1) * LANES] = o.astype(o_ref.dtype)


def _block_expander(n_blk, n_keys):
    key_blk = np.arange(n_keys) // SEL_BLOCK
    e = (np.arange(n_blk)[:, None] == key_blk[None, :]).astype(np.float32)
    return jnp.asarray(e.reshape(n_blk, n_keys // TQ, TQ).transpose(1, 0, 2), BF16)


def _window_edge_tile():
    r = np.arange(TQ)
    return jnp.asarray(np.where(r[None, :] > r[:, None], 0.0, NEG).astype(np.float32))


def _nsa_prompt(m, ck, cv, tcmp, tdiag, tprev, b, s):
    nq = s // TQ
    n_cmp = s // CMP_BLOCK
    n_blk = s // SEL_BLOCK
    n_sel = min(N_SEL, n_blk)
    assert WINDOW == 2 * TQ
    qblk = lambda width: pl.BlockSpec((1, TQ, width), lambda bi, i: (bi, i, 0))
    seq = lambda rows, width: pl.BlockSpec((1, rows, width), lambda bi, i: (bi, 0, 0))
    r3 = lambda a: a.reshape(b, s, a.shape[-1])
    return pl.pallas_call(
        functools.partial(_nsa_prompt_kernel, n_sel=n_sel),
        grid=(b, nq),
        in_specs=[qblk(1024), qblk(LANES), seq(n_cmp, LANES), seq(n_cmp, LANES),
                  seq(s, LANES), seq(s, LANES), seq(s, LANES), seq(s, LANES),
                  pl.BlockSpec((1, MIX_HEADS, TQ, n_cmp), lambda bi, i: (i, 0, 0, 0)),
                  _full((MIX_HEADS, TQ, TQ)), _full((MIX_HEADS, TQ, TQ)), _full((TQ, TQ)),
                  _full((nq, n_blk, TQ))],
        out_specs=qblk(1024),
        out_shape=jax.ShapeDtypeStruct((b, s, MIX_HEADS * LANES), BF16),
        scratch_shapes=[pltpu.VMEM((GROUP_HEADS * TQ, LANES), BF16),
                        pltpu.VMEM((GROUP_HEADS * TQ, 1), F32),
                        pltpu.VMEM((GROUP_HEADS * TQ, 1), F32),
                        pltpu.VMEM((GROUP_HEADS * TQ, LANES), F32)],
        compiler_params=_cparams("arbitrary", "arbitrary"),
        name="nsa_prompt",
    )(r3(m["qn"]), r3(m["ng"]), ck, cv, r3(m["sk"]), r3(m["sv"]), r3(m["wk"]), r3(m["wv"]),
      tcmp, tdiag, tprev, _window_edge_tile(), _block_expander(n_blk, s))


def _merge_kernel(h_ref, oa_ref, ob_ref, ga_ref, gb_ref, wa_ref, wb_ref, wo_ref, o_ref):
    merged = ga_ref[...] * _dot(oa_ref[...], wa_ref[...]) + gb_ref[...] * _dot(ob_ref[...], wb_ref[...])
    o_ref[...] = h_ref[...] + _dot(merged.astype(BF16), wo_ref[...])


def _pad_head_rows(w):
    return _pad_heads(w.T, True).T.astype(BF16)


def _merge(h, oa, ob, ga, gb, wa, wb, wo, tm):
    n, d = h.shape
    row = lambda width: pl.BlockSpec((tm, width), lambda i: (i, 0))
    return pl.pallas_call(
        _merge_kernel,
        grid=(n // tm,),
        in_specs=[row(d), row(1024), row(1024), row(d), row(d),
                  _full(wa.shape), _full(wb.shape), _full(wo.shape)],
        out_specs=row(d),
        out_shape=jax.ShapeDtypeStruct((n, d), F32),
        compiler_params=_cparams("arbitrary"),
        name="merge",
    )(h, oa, ob, ga, gb, wa, wb, wo)


PAGES_PER_STEP = 8


def _page_specs(shape_tail, step_of):
    nd = len(shape_tail)

    def spec(pp):
        def imap(b, s, pt):
            return (pt[b, step_of(s) * PAGES_PER_STEP + pp],) + (0,) * nd
        return pl.BlockSpec((1,) + shape_tail, imap)

    return [spec(pp) for pp in range(PAGES_PER_STEP)]


def _pad_rows(x, pad_ref):
    pad_ref[...] = jnp.zeros_like(pad_ref)
    pad_ref[0:DS_PAD, :] = x
    return pad_ref[...].astype(BF16)


def _dsa_sample_kernel(pt_ref, qa_ref, iqa_ref, iw_ref, iknew_ref, kvnew_ref, *rest, n_pages, k_top):
    npp = PAGES_PER_STEP
    idx_refs, kv_refs = rest[:npp], rest[npp:2 * npp]
    (slast_ref, snew_ref, utri_ref, o_ref, keys_ref, qs_ref, iqs_ref, wb_ref, thr_ref, need_ref,
     run_ref, pad_ref, m_ref, l_ref, acc_ref) = rest[2 * npp:]
    s = pl.program_id(1)
    ns = n_pages // npp
    nh = MIX_HEADS
    rq = DS_PAD

    def score(ik_chunk):
        sd = _dot_nt(iqs_ref[:, :HEAD_DIM].astype(BF16), ik_chunk)
        sc = jnp.zeros((rq, PAGE), F32)
        for h in range(IDX_HEADS):
            sc = sc + jnp.maximum(sd[h * rq:(h + 1) * rq], 0.0) * wb_ref[h]
        return sc

    @pl.when(s == 0)
    def _():
        _stack_heads(qa_ref[...], qs_ref, nh, rq)
        _stack_heads(iqa_ref[...], iqs_ref, IDX_HEADS, rq)
        iw = iw_ref[...]
        for h in range(IDX_HEADS):
            wb_ref[h] = jnp.broadcast_to(iw[:, MISC_IW + h:MISC_IW + h + 1], (rq, LANES))

    @pl.when(s < ns)
    def _():
        for pp in range(npp):
            keys_ref[s * npp + pp] = _sortable(score(idx_refs[pp][0].astype(BF16)))

    @pl.when(s == ns - 1)
    def _():
        sc = score(_pad_rows(iknew_ref[...], pad_ref)[:, :HEAD_DIM])
        sc = jnp.where(_iota2((rq, PAGE), 0) >= _iota2((rq, PAGE), 1), sc, NEG)
        keys_ref[n_pages] = _sortable(sc)

        def count_ge(cand):
            def body(c, acc):
                return acc + (keys_ref[c] >= cand).astype(I32)
            return _lane_sum_i32(lax.fori_loop(0, n_pages + 1, body, jnp.zeros((rq, LANES), I32)))

        thr = _kth_largest(count_ge, rq, k_top)
        thr_ref[...] = thr
        need_ref[...] = (k_top - count_ge(thr + 1)).astype(F32)
        _flash_init(m_ref, l_ref, acc_ref)
        run_ref[...] = jnp.zeros_like(run_ref)

    def attend(c, k, v, tile_ref, tile_on):
        sel = _topk_select_bias(keys_ref[c], thr_ref[...], need_ref[...], run_ref, utri_ref[...])
        lg = _dot_nt(qs_ref[...].astype(BF16), k)
        parts = []
        for h in range(nh):
            parts.append(lg[h * rq:(h + 1) * rq] + sel + jnp.where(tile_on, tile_ref[h], 0.0))
        _flash_update(jnp.concatenate(parts, axis=0), v, m_ref, l_ref, acc_ref)

    @pl.when(s >= ns)
    def _():
        for pp in range(npp):
            c = (s - ns) * npp + pp
            page = kv_refs[pp][0]
            attend(c, page[:, :LANES].astype(BF16), page[:, LANES:].astype(BF16), slast_ref, c == n_pages - 1)

    @pl.when(s == 2 * ns - 1)
    def _():
        kvn = kvnew_ref[...]
        kn = _pad_rows(kvn[:, :LANES], pad_ref)
        vn = _pad_rows(kvn[:, LANES:], pad_ref)
        attend(n_pages, kn, vn, snew_ref, True)
        inv = 1.0 / l_ref[...]
        for h in range(nh):
            rows = slice(h * rq, (h + 1) * rq)
            o_ref[:, h * LANES:(h + 1) * LANES] = acc_ref[rows, :] * inv[rows]


def _dsa_sample(m, cache_idx, cache_kv, page_table, slast, snew, db):
    n_pages = page_table.shape[1]
    ns = n_pages // PAGES_PER_STEP
    assert ns * PAGES_PER_STEP == n_pages
    k_top = min(IDX_TOPK, (n_pages * PAGE + 4) // 4)
    n_pool = cache_idx.shape[0]
    row = lambda width: pl.BlockSpec((DS_PAD, width), lambda b, s, pt: (b, 0))
    const = lambda shape: pl.BlockSpec(shape, lambda b, s, pt: (0,) * len(shape))
    rows = MIX_HEADS * DS_PAD
    grid_spec = pltpu.PrefetchScalarGridSpec(
        num_scalar_prefetch=1,
        grid=(db, 2 * ns),
        in_specs=[row(1024), row(1024), row(LANES), row(LANES), row(2 * LANES)]
        + _page_specs((PAGE, HEAD_DIM), lambda s: jnp.minimum(s, ns - 1))
        + _page_specs((PAGE, 2 * LANES), lambda s: jnp.maximum(s - ns, 0))
        + [const((MIX_HEADS, DS_PAD, PAGE)), const((MIX_HEADS, DS_PAD, PAGE)), const((PAGE, PAGE))],
        out_specs=row(1024),
        scratch_shapes=[pltpu.VMEM((n_pages + 1, DS_PAD, PAGE), I32),
                        pltpu.VMEM((rows, LANES), F32), pltpu.VMEM((rows, LANES), F32),
                        pltpu.VMEM((IDX_HEADS, DS_PAD, LANES), F32),
                        pltpu.VMEM((DS_PAD, LANES), I32), pltpu.VMEM((DS_PAD, LANES), F32),
                        pltpu.VMEM((DS_PAD, LANES), F32), pltpu.VMEM((PAGE, LANES), F32),
                        pltpu.VMEM((rows, 1), F32), pltpu.VMEM((rows, 1), F32),
                        pltpu.VMEM((rows, LANES), F32)])
    idx_pages = cache_idx.reshape(n_pool, PAGE, HEAD_DIM)
    kv_pages = cache_kv.reshape(n_pool, PAGE, 2 * LANES)
    return pl.pallas_call(
        functools.partial(_dsa_sample_kernel, n_pages=n_pages, k_top=k_top),
        grid_spec=grid_spec,
        out_shape=jax.ShapeDtypeStruct((db * DS_PAD, 1024), F32),
        compiler_params=_cparams("arbitrary", "arbitrary"),
        name="dsa_sample",
    )(page_table, m["qd"], m["iq"], m["iw"], m["ika"].astype(F32), m["dkv"],
      *([idx_pages] * PAGES_PER_STEP), *([kv_pages] * PAGES_PER_STEP), slast, snew, _upper_tri(PAGE))


def _compress_sample_kernel(pt_ref, *rest, n_pages):
    npp = PAGES_PER_STEP
    page_refs = rest[:npp]
    wbig_ref, gck_ref, s64_ref, ck_ref, cv_ref, x_ref, top_ref, bot_ref = rest[npp:]
    s = pl.program_id(1)
    rows_per_page = 8
    for pp in range(npp):
        r0 = pl.multiple_of((s * npp + pp) * rows_per_page, rows_per_page)
        x_ref[pl.ds(r0, rows_per_page), :] = page_refs[pp][0]

    @pl.when(s == n_pages // npp - 1)
    def _():
        kh = wbig_ref.shape[0] // 2
        xb = x_ref[...].astype(BF16)
        top = _dot(xb, wbig_ref[0:kh, :])
        bot = _dot(xb, wbig_ref[kh:, :])
        for j in range(2):
            top_ref[j] = top[:, j * LANES:(j + 1) * LANES]
            bot_ref[j] = bot[:, j * LANES:(j + 1) * LANES]
        half = n_pages * 2
        for p in range(2):
            ck, cv = [top_ref[j, pl.ds(2 * p, half, stride=4), :] + bot_ref[j, pl.ds(2 * p + 1, half, stride=4), :]
                      for j in range(2)]
            ck_ref[0, p * half:(p + 1) * half, :] = _norm64(ck, gck_ref[...], s64_ref[...]).astype(BF16)
            cv_ref[0, p * half:(p + 1) * half, :] = cv.astype(BF16)


def _compress_sample(cache_cmp, page_table, cw, db):
    wbig, gck = cw
    n_pages = page_table.shape[1]
    n_pool = cache_cmp.shape[0]
    n_cmp = n_pages * PAGE // CMP_BLOCK
    width = PAGE * 2 * LANES // 8
    const = lambda shape: pl.BlockSpec(shape, lambda b, s, pt: (0,) * len(shape))
    out = jax.ShapeDtypeStruct((db, n_cmp, LANES), BF16)
    grid_spec = pltpu.PrefetchScalarGridSpec(
        num_scalar_prefetch=1,
        grid=(db, n_pages // PAGES_PER_STEP),
        in_specs=_page_specs((8, width), lambda s: s)
        + [const(wbig.shape), const((1, LANES)), const((LANES, LANES))],
        out_specs=[pl.BlockSpec((1, n_cmp, LANES), lambda b, s, pt: (b, 0, 0))] * 2,
        scratch_shapes=[pltpu.VMEM((n_pages * 8, width), F32),
                        pltpu.VMEM((2, n_pages * 8, LANES), F32),
                        pltpu.VMEM((2, n_pages * 8, LANES), F32)])
    pages = cache_cmp.reshape(n_pool, 8, width)
    return pl.pallas_call(
        functools.partial(_compress_sample_kernel, n_pages=n_pages),
        grid_spec=grid_spec,
        out_shape=[out, out],
        compiler_params=_cparams("arbitrary", "arbitrary"),
        name="compress_sample",
    )(page_table, *([pages] * PAGES_PER_STEP), wbig, gck, _seg_matrix())


def _nsa_sample_kernel(pt_ref, qa_ref, ng_ref, ck_ref, cv_ref, win_ref, sknew_ref, wknew_ref, *rest,
                       n_pages, n_sel):
    npp = PAGES_PER_STEP
    sel_refs = rest[:npp]
    (scmp_ref, slast_ref, snew_ref, swin_ref, o_ref,
     qs_ref, picked_ref, oc_ref, pad_ref, m_ref, l_ref, acc_ref) = rest[npp:]
    s = pl.program_id(1)
    ns = n_pages // npp
    nh = MIX_HEADS
    gh = GROUP_HEADS
    rq = DS_PAD
    n_cmp = ck_ref.shape[1]
    n_past_blk = n_cmp // 2
    nb = picked_ref.shape[2]

    def tiled(lg, tile_ref, extra=None):
        parts = []
        for h in range(nh):
            part = lg[h * rq:(h + 1) * rq] + tile_ref[h]
            if extra is not None:
                part = part + extra[h // gh]
            parts.append(part)
        return jnp.concatenate(parts, axis=0)

    @pl.when(s == 0)
    def _():
        _stack_heads(qa_ref[...], qs_ref, nh, rq)
        lc = tiled(_dot_nt(qs_ref[...].astype(BF16), ck_ref[0]), scmp_ref)
        pc = _softmax_valid(lc)
        oc_ref[...] = _dot(pc.astype(BF16), cv_ref[0])
        blk = _iota2((rq, nb), 1)
        forced = (blk == 0) | (blk == n_past_blk) | (blk == n_past_blk - 1)
        for g in range(N_GROUPS):
            imp = pc[g * gh * rq:(g * gh + 1) * rq]
            for r in range(1, gh):
                imp = imp + pc[(g * gh + r) * rq:(g * gh + r + 1) * rq]
            imp = imp[:, :n_past_blk] + imp[:, n_past_blk:]
            imp = jnp.concatenate([imp, jnp.zeros((rq, nb - n_past_blk), F32)], axis=1)
            score = jnp.where(forced, BIG, jnp.where(blk < n_past_blk, imp, -jnp.inf))
            picked_ref[g] = _pick_blocks(score, n_sel)
        _flash_init(m_ref, l_ref, acc_ref)

    blk = _iota2((rq, nb), 1)
    lane = _iota2((rq, PAGE), 1)
    for pp in range(npp):
        c = s * npp + pp
        page = sel_refs[pp][0]
        keep = []
        for g in range(N_GROUPS):
            pk = picked_ref[g]
            lo = jnp.sum(jnp.where(blk == 2 * c, pk, 0.0), axis=-1, keepdims=True)
            hi = jnp.sum(jnp.where(blk == 2 * c + 1, pk, 0.0), axis=-1, keepdims=True)
            keep.append(jnp.where(jnp.where(lane < SEL_BLOCK, lo, hi) > 0.5, 0.0, NEG))
        lg = _dot_nt(qs_ref[...].astype(BF16), page[:, :LANES].astype(BF16))
        parts = []
        for h in range(nh):
            parts.append(lg[h * rq:(h + 1) * rq] + keep[h // gh]
                         + jnp.where(c == n_pages - 1, slast_ref[h], 0.0))
        _flash_update(jnp.concatenate(parts, axis=0), page[:, LANES:].astype(BF16), m_ref, l_ref, acc_ref)

    @pl.when(s == ns - 1)
    def _():
        ng = ng_ref[...]
        skn = sknew_ref[...]
        kn = _pad_rows(skn[:, :LANES], pad_ref)
        vn = _pad_rows(skn[:, LANES:], pad_ref)
        _flash_update(tiled(_dot_nt(qs_ref[...].astype(BF16), kn), snew_ref), vn, m_ref, l_ref, acc_ref)
        o_s = acc_ref[...] * (1.0 / l_ref[...])
        _flash_init(m_ref, l_ref, acc_ref)
        win = win_ref[0]
        lg = _dot_nt(qs_ref[...].astype(BF16), win[:, :LANES].astype(BF16))
        _flash_update(tiled(lg, swin_ref), win[:, LANES:].astype(BF16), m_ref, l_ref, acc_ref)
        wkn = wknew_ref[...]
        kn = _pad_rows(wkn[:, :LANES], pad_ref)
        vn = _pad_rows(wkn[:, LANES:], pad_ref)
        _flash_update(tiled(_dot_nt(qs_ref[...].astype(BF16), kn), snew_ref), vn, m_ref, l_ref, acc_ref)
        o_w = acc_ref[...] * (1.0 / l_ref[...])
        o_c = oc_ref[...]
        for h in range(nh):
            sl = slice(h * rq, (h + 1) * rq)
            gate = lambda j: ng[:, MISC_NG + 3 * h + j:MISC_NG + 3 * h + j + 1]
            o_ref[:, h * LANES:(h + 1) * LANES] = gate(0) * o_c[sl] + gate(1) * o_s[sl] + gate(2) * o_w[sl]


def _nsa_sample(m, ck, cv, cache_sel, state_win, page_table, scmp, slast, snew, swin, db):
    n_pages = page_table.shape[1]
    ns = n_pages // PAGES_PER_STEP
    n_pool = cache_sel.shape[0]
    n_cmp = ck.shape[1]
    n_sblk = n_cmp // 2 + 1
    n_sel = min(N_SEL, n_sblk)
    nb = -(-n_sblk // LANES) * LANES
    assert state_win.shape[1] == WINDOW
    row = lambda width: pl.BlockSpec((DS_PAD, width), lambda b, s, pt: (b, 0))
    seq = lambda r, width: pl.BlockSpec((1, r, width), lambda b, s, pt: (b, 0, 0))
    const = lambda shape: pl.BlockSpec(shape, lambda b, s, pt: (0,) * len(shape))
    rows = MIX_HEADS * DS_PAD
    grid_spec = pltpu.PrefetchScalarGridSpec(
        num_scalar_prefetch=1,
        grid=(db, ns),
        in_specs=[row(1024), row(LANES), seq(n_cmp, LANES), seq(n_cmp, LANES), seq(WINDOW, 2 * LANES),
                  row(2 * LANES), row(2 * LANES)]
        + _page_specs((PAGE, 2 * LANES), lambda s: s)
        + [const((MIX_HEADS, DS_PAD, n_cmp)), const((MIX_HEADS, DS_PAD, PAGE)),
           const((MIX_HEADS, DS_PAD, PAGE)), const((MIX_HEADS, DS_PAD, WINDOW))],
        out_specs=row(1024),
        scratch_shapes=[pltpu.VMEM((rows, LANES), F32), pltpu.VMEM((N_GROUPS, DS_PAD, nb), F32),
                        pltpu.VMEM((rows, LANES), F32), pltpu.VMEM((PAGE, LANES), F32),
                        pltpu.VMEM((rows, 1), F32), pltpu.VMEM((rows, 1), F32),
                        pltpu.VMEM((rows, LANES), F32)])
    sel_pages = cache_sel.reshape(n_pool, PAGE, 2 * LANES)
    return pl.pallas_call(
        functools.partial(_nsa_sample_kernel, n_pages=n_pages, n_sel=n_sel),
        grid_spec=grid_spec,
        out_shape=jax.ShapeDtypeStruct((db * DS_PAD, 1024), F32),
        compiler_params=_cparams("arbitrary", "arbitrary"),
        name="nsa_sample",
    )(page_table, m["qn"], m["ng"], ck, cv, state_win.reshape(db, WINDOW, 2 * LANES), m["skv"], m["wkv"],
      *([sel_pages] * PAGES_PER_STEP), scmp, slast, snew, swin)


def kernel(x_prompt, x_sample, cache_dsa_kv, cache_dsa_idx_k, cache_nsa_cmp_kv, cache_nsa_sel_kv,
           state_nsa_win_kv, page_table, rel_bias, ffn1_norm, ffn1_w_gu, ffn1_w_down, mix_norm, w_in,
           dsa_q_norm, dsa_k_norm, idx_k_norm, nsa_q_norm, nsa_cmp_k_norm, nsa_sel_k_norm, nsa_win_k_norm,
           nsa_w_phi, w_dsa_o, w_nsa_o, w_out, ffn2_norm, ffn2_w_gu, ffn2_w_down):
    b, s, d = x_prompt.shape
    db, ds, _ = x_sample.shape
    depth = ffn1_norm.shape[0]
    n_pages = page_table.shape[1]
    past = n_pages * PAGE
    assert s % TQ == 0 and ds <= DS_PAD and ds < CMP_BLOCK and past >= WINDOW
    n_p, n_s = b * s, db * DS_PAD
    tm_p, tm_s = min(512, n_p), min(512, n_s)

    hp = x_prompt.reshape(n_p, d)
    hs = jnp.pad(x_sample, ((0, 0), (0, DS_PAD - ds), (0, 0))).reshape(n_s, d)
    dsa_h = slice(0, MIX_HEADS)
    nsa_h = slice(MIX_HEADS, 2 * MIX_HEADS)
    tdiag, tprev, slast, snew, swin, scmp = _bias_tiles(rel_bias, past // CMP_BLOCK)
    tcmp = _cmp_tiles(rel_bias, MIX_HEADS, s // TQ, s // CMP_BLOCK)

    outs_p = [[] for _ in range(5)]
    outs_s = [[] for _ in range(5)]
    for l in range(depth):
        pw = _proj_weights(w_in[l], dsa_q_norm[l], dsa_k_norm[l], idx_k_norm[l], nsa_q_norm[l],
                           nsa_sel_k_norm[l], nsa_win_k_norm[l])
        cw = _compress_weights(nsa_w_phi[l], nsa_cmp_k_norm[l])
        wa, wb = _pad_head_rows(w_dsa_o[l]), _pad_head_rows(w_nsa_o[l])
        wo = w_out[l].astype(BF16)

        hp = _ffn(hp, ffn1_norm[l], ffn1_w_gu[l], ffn1_w_down[l], tm_p)
        mp = _proj(hp, mix_norm[l], pw, tm_p, BF16)
        oa = _dsa_prompt(mp, tdiag[dsa_h], tprev[dsa_h], b, s)
        ck, cv = _compress_prompt(mp["ckv"], cw, b, s)
        ob = _nsa_prompt(mp, ck, cv, tcmp, tdiag[nsa_h], tprev[nsa_h], b, s)
        hp = _merge(hp, oa.reshape(n_p, -1), ob.reshape(n_p, -1), mp["ga"], mp["gb"], wa, wb, wo, tm_p)
        hp = _ffn(hp, ffn2_norm[l], ffn2_w_gu[l], ffn2_w_down[l], tm_p)
        kv5 = lambda a: a.reshape(b, s, 2, N_GROUPS, HEAD_DIM)
        win_len = min(WINDOW, s)
        for lst, val in zip(outs_p, (kv5(mp["dkv"]), mp["ik"].reshape(b, s, HEAD_DIM), kv5(mp["ckv"]),
                                     kv5(mp["skv"]), kv5(mp["wkv"])[:, s - win_len:])):
            lst.append(val)

        hs = _ffn(hs, ffn1_norm[l], ffn1_w_gu[l], ffn1_w_down[l], tm_s)
        ms = _proj(hs, mix_norm[l], pw, tm_s, F32)
        oa_s = _dsa_sample(ms, cache_dsa_idx_k[l], cache_dsa_kv[l], page_table, slast[dsa_h], snew[dsa_h], db)
        ck_s, cv_s = _compress_sample(cache_nsa_cmp_kv[l], page_table, cw, db)
        ob_s = _nsa_sample(ms, ck_s, cv_s, cache_nsa_sel_kv[l], state_nsa_win_kv[l], page_table,
                           scmp[nsa_h], slast[nsa_h], snew[nsa_h], swin[nsa_h], db)
        hs = _merge(hs, oa_s.astype(BF16), ob_s.astype(BF16), ms["ga"], ms["gb"], wa, wb, wo, tm_s)
        hs = _ffn(hs, ffn2_norm[l], ffn2_w_gu[l], ffn2_w_down[l], tm_s)
        new5 = lambda a: a.reshape(db, DS_PAD, -1)[:, :ds].reshape(db, ds, 2, N_GROUPS, HEAD_DIM)
        win = jnp.concatenate([state_nsa_win_kv[l], new5(ms["wkv"])], axis=1)
        win = win[:, win.shape[1] - min(WINDOW, past + ds):]
        for lst, val in zip(outs_s, (new5(ms["dkv"]), ms["ik"].reshape(db, DS_PAD, HEAD_DIM)[:, :ds],
                                     new5(ms["ckv"]), new5(ms["skv"]), win)):
            lst.append(val)

    y_p = hp.reshape(b, s, d)
    y_s = hs.reshape(db, DS_PAD, d)[:, :ds]
    return (y_p, y_s) + tuple(jnp.stack(o) for o in outs_p) + tuple(jnp.stack(o) for o in outs_s)
```

```python
import functools
import math

import numpy as np
import jax
import jax.numpy as jnp
from jax import lax
from jax.experimental import pallas as pl
from jax.experimental.pallas import tpu as pltpu

F32 = jnp.float32
BF16 = jnp.bfloat16
I32 = jnp.int32

HEAD_DIM = 64
LANES = 128
SUBLANES = 8
N_GROUPS = 2
GROUP_HEADS = 4
MIX_HEADS = N_GROUPS * GROUP_HEADS
IDX_HEADS = 8
IDX_TOPK = 256
CMP_BLOCK = 32
SEL_BLOCK = 64
N_SEL = 16
WINDOW = 512
NUM_BUCKETS = 32
MAX_EXACT = 16
MAX_DISTANCE = 128
PAGE = 128
EPS = 1e-6
NEG = -1e30
BIG = 1e30
TQ = 256
DS_PAD = 8
PAGES_PER_STEP = 8
VMEM_LIMIT = 56 * 1024 * 1024

INT_MIN = -(2 ** 31)


def _bucket_thresholds():
    n = np.arange(MAX_EXACT, 4 * MAX_DISTANCE)
    ratio = np.log(n.astype(np.float32) / np.float32(MAX_EXACT)) / np.float32(math.log(MAX_DISTANCE / MAX_EXACT))
    big = np.minimum(MAX_EXACT + (ratio * np.float32(NUM_BUCKETS - MAX_EXACT)).astype(np.int32), NUM_BUCKETS - 1)
    return [int(n[np.argmax(big >= b)]) for b in range(MAX_EXACT + 1, NUM_BUCKETS)]


BUCKET_THR = _bucket_thresholds()


def _dot(a, b):
    return jnp.dot(a, b, preferred_element_type=F32)


def _dot_nt(a, b):
    return lax.dot_general(a, b, (((1,), (1,)), ((), ())), preferred_element_type=F32)


def _cparams(*sem):
    return pltpu.CompilerParams(dimension_semantics=sem, vmem_limit_bytes=VMEM_LIMIT)


def _full(shape):
    nd = len(shape)
    return pl.BlockSpec(shape, lambda *_: (0,) * nd)


def _iota2(shape, axis):
    return lax.broadcasted_iota(I32, shape, axis)


def _rms_rows(x, g):
    return x * lax.rsqrt(jnp.mean(x * x, axis=-1, keepdims=True) + EPS) * g


def _norm64(z, gain, s64):
    outs = []
    for j in range(z.shape[1] // LANES):
        zb = z[:, j * LANES:(j + 1) * LANES]
        sq = zb * zb
        hi = sq.astype(BF16)
        lo = (sq - hi.astype(F32)).astype(BF16)
        ms = (_dot(hi, s64) + _dot(lo, s64)) * (1.0 / HEAD_DIM)
        outs.append(zb * lax.rsqrt(ms + EPS) * gain[:, j * LANES:(j + 1) * LANES])
    return outs[0] if len(outs) == 1 else jnp.concatenate(outs, axis=1)


def _seg_matrix():
    r = np.arange(LANES) // HEAD_DIM
    return jnp.asarray((r[:, None] == r[None, :]).astype(np.float32), BF16)


def _ffn_kernel(x_ref, g_ref, wg_ref, wu_ref, wd_ref, o_ref, xn_ref, acc_ref):
    x = x_ref[...]
    xn_ref[...] = _rms_rows(x, g_ref[...]).astype(BF16)
    acc_ref[...] = jnp.zeros_like(acc_ref)

    def body(c, carry):
        xn = xn_ref[...]
        a = _dot(xn, wg_ref[c])
        b = _dot(xn, wu_ref[c])
        act = (a * jax.nn.sigmoid(a) * b).astype(BF16)
        acc_ref[...] += _dot(act, wd_ref[c])
        return carry

    lax.fori_loop(0, wg_ref.shape[0], body, 0)
    o_ref[...] = x + 0.5 * acc_ref[...]


def _ffn(x, g, w_gu, w_dn, tm):
    n, d = x.shape
    d_ff = w_dn.shape[0]
    fc = 256
    nc = d_ff // fc
    assert nc * fc == d_ff and n % tm == 0
    wg = w_gu[:, :d_ff].reshape(d, nc, fc).transpose(1, 0, 2).astype(BF16)
    wu = w_gu[:, d_ff:].reshape(d, nc, fc).transpose(1, 0, 2).astype(BF16)
    wd = w_dn.reshape(nc, fc, d).astype(BF16)
    return pl.pallas_call(
        _ffn_kernel,
        grid=(n // tm,),
        in_specs=[pl.BlockSpec((tm, d), lambda i: (i, 0)), _full((1, d)),
                  _full((nc, d, fc)), _full((nc, d, fc)), _full((nc, fc, d))],
        out_specs=pl.BlockSpec((tm, d), lambda i: (i, 0)),
        out_shape=jax.ShapeDtypeStruct((n, d), F32),
        scratch_shapes=[pltpu.VMEM((tm, d), BF16), pltpu.VMEM((tm, d), F32)],
        compiler_params=_cparams("arbitrary"),
        name="ffn",
    )(x, g.reshape(1, d), wg, wu, wd)


_W_QD, _W_IQ, _W_QN = 0, 1024, 2048
_W_DKV, _W_CKV, _W_SKV, _W_WKV = 3072, 3328, 3584, 3840
_W_MG, _W_IK, _W_MISC, _W_TOTAL = 4096, 6144, 6272, 6400
MISC_IW = 0
MISC_NG = 8


def _proj_kernel(h_ref, gm_ref, w_ref, gqd_ref, gqn_ref, gk_ref, gik_ref, s64_ref, *out_refs, with_vt):
    refs = list(out_refs)
    hn_ref = refs.pop()
    qd_ref, iq_ref, qn_ref = refs[:3]
    kv_refs = refs[3:3 + (4 if with_vt else 3) * 3]
    ckv_ref, ckvt_ref, ga_ref, gb_ref, ikt_ref, ika_ref, iw_ref, ng_ref = refs[len(refs) - 8:]
    hn_ref[...] = _rms_rows(h_ref[...], gm_ref[...]).astype(BF16)
    s64 = s64_ref[...]
    tm = h_ref.shape[0]

    def seg(a, b):
        return _dot(hn_ref[...], w_ref[:, a:b])

    qd_ref[...] = _norm64(seg(_W_QD, _W_IQ), gqd_ref[...], s64).astype(qd_ref.dtype)
    iq_ref[...] = (seg(_W_IQ, _W_QN) * (HEAD_DIM ** -0.5)).astype(iq_ref.dtype)
    qn_ref[...] = _norm64(seg(_W_QN, _W_DKV), gqn_ref[...], s64).astype(qn_ref.dtype)
    per = 4 if with_vt else 3
    for j, a in enumerate((_W_DKV, _W_SKV, _W_WKV)):
        kvt_ref, k_ref, v_ref = kv_refs[j * per:j * per + 3]
        z = seg(a, a + 2 * LANES)
        kn = _norm64(z[:, :LANES], gk_ref[:, j * LANES:(j + 1) * LANES], s64)
        vt = z[:, LANES:].T
        kvt_ref[0, :LANES, :] = kn.T
        kvt_ref[0, LANES:, :] = vt
        k_ref[...] = kn.astype(BF16)
        v_ref[...] = z[:, LANES:].astype(BF16)
        if with_vt:
            vt_ref = kv_refs[j * per + 3]
            for cc in range(tm // TQ):
                vt_ref[0, cc] = vt[:, cc * TQ:(cc + 1) * TQ].astype(BF16)
    zc = seg(_W_CKV, _W_SKV)
    ckv_ref[...] = zc
    ckvt_ref[0, :LANES, :] = zc[:, :LANES].T
    ckvt_ref[0, LANES:, :] = zc[:, LANES:].T
    half = (_W_IK - _W_MG) // 2
    ga_ref[...] = jax.nn.sigmoid(seg(_W_MG, _W_MG + half))
    gb_ref[...] = jax.nn.sigmoid(seg(_W_MG + half, _W_IK))
    ikn = _norm64(seg(_W_IK, _W_MISC), gik_ref[...], s64)
    ikt_ref[0] = ikn.T[:HEAD_DIM, :]
    ika_ref[...] = ikn.astype(BF16)
    misc = seg(_W_MISC, _W_TOTAL)
    iw_ref[...] = misc * (IDX_HEADS ** -0.5)
    ng_ref[...] = jax.nn.sigmoid(misc)


def _pad_heads(w, by_group):
    d = w.shape[0]
    w = w.reshape(d, MIX_HEADS, HEAD_DIM)
    z = jnp.zeros_like(w)
    if by_group:
        grp = (jnp.arange(MIX_HEADS) // GROUP_HEADS)[None, :, None]
        lo, hi = jnp.where(grp == 0, w, z), jnp.where(grp == 1, w, z)
    else:
        lo, hi = w, z
    return jnp.concatenate([lo, hi], axis=-1).reshape(d, MIX_HEADS * LANES)


def _proj_weights(w_in, dsa_q_norm, dsa_k_norm, idx_k_norm, nsa_q_norm, nsa_sel_k_norm, nsa_win_k_norm):
    d = w_in.shape[0]
    sizes = (512, 256, 512, 8, 64, 512, 256, 256, 256, 24, 2 * d)
    offs = np.concatenate([[0], np.cumsum(sizes)])
    dq, dkv, iq, iw, ik, nq, ckv, skv, wkv, ng, mg = [w_in[:, offs[j]:offs[j + 1]] for j in range(len(sizes))]
    misc = jnp.concatenate([iw, ng, jnp.zeros((d, LANES - 32), F32)], axis=1)
    ikp = jnp.concatenate([ik, jnp.zeros((d, LANES - HEAD_DIM), F32)], axis=1)
    w = jnp.concatenate([_pad_heads(dq, True), _pad_heads(iq, False), _pad_heads(nq, True),
                         dkv, ckv, skv, wkv, mg, ikp, misc], axis=1).astype(BF16)
    assert w.shape[1] == _W_TOTAL
    scale = HEAD_DIM ** -0.5
    gqd = _pad_heads(jnp.tile(dsa_q_norm * scale, MIX_HEADS)[None, :], True)
    gqn = _pad_heads(jnp.tile(nsa_q_norm * scale, MIX_HEADS)[None, :], True)
    gk = jnp.concatenate([jnp.tile(g, 2) for g in (dsa_k_norm, nsa_sel_k_norm, nsa_win_k_norm)])[None, :]
    gik = jnp.concatenate([idx_k_norm, jnp.zeros((LANES - HEAD_DIM,), F32)])[None, :]
    return w, gqd, gqn, gk, gik


def _proj(h, g_mix, pw, nb, tm, q_dtype, with_vt):
    n, d = h.shape
    s = n // nb
    nt = s // tm
    w, gqd, gqn, gk, gik = pw
    row = lambda width: pl.BlockSpec((tm, width), lambda b, j: (b * nt + j, 0))
    colmaj = lambda width: pl.BlockSpec((1, width, tm), lambda b, j: (b, 0, j))
    const = lambda shape: pl.BlockSpec(shape, lambda b, j: (0,) * len(shape))
    tok = lambda width, dt: (row(width), jax.ShapeDtypeStruct((n, width), dt))
    feat = lambda width: (colmaj(width), jax.ShapeDtypeStruct((nb, width, s), F32))
    outs = [("qd", tok(1024, q_dtype)), ("iq", tok(1024, q_dtype)), ("qn", tok(1024, q_dtype))]
    for p in ("d", "s", "w"):
        outs += [(p + "kvt", feat(256)), (p + "k", tok(LANES, BF16)), (p + "v", tok(LANES, BF16))]
        if with_vt:
            outs.append((p + "vt", (pl.BlockSpec((1, tm // TQ, LANES, TQ), lambda b, j: (b, j, 0, 0)),
                                    jax.ShapeDtypeStruct((nb, s // TQ, LANES, TQ), BF16))))
    outs += [("ckv", tok(256, F32)), ("ckvt", feat(256)), ("ga", tok(d, F32)), ("gb", tok(d, F32)),
             ("ikt", feat(HEAD_DIM)), ("ika", tok(LANES, BF16)), ("iw", tok(LANES, F32)), ("ng", tok(LANES, F32))]
    res = pl.pallas_call(
        functools.partial(_proj_kernel, with_vt=with_vt),
        grid=(nb, nt),
        in_specs=[row(d), const((1, d)), const(w.shape), const(gqd.shape), const(gqn.shape),
                  const(gk.shape), const(gik.shape), const((LANES, LANES))],
        out_specs=[spec for _, (spec, _) in outs],
        out_shape=[shape for _, (_, shape) in outs],
        scratch_shapes=[pltpu.VMEM((tm, d), BF16)],
        compiler_params=_cparams("arbitrary", "arbitrary"),
        name="proj",
    )(h, g_mix.reshape(1, d), w, gqd, gqn, gk, gik, _seg_matrix())
    return {name: r for (name, _), r in zip(outs, res)}


def _bias_tile(dist, valid, tab_ref, h):
    n = jnp.maximum(dist, 0)
    big = jnp.full(n.shape, MAX_EXACT, I32)
    for thr in BUCKET_THR:
        big = big + (n >= thr).astype(I32)
    bucket = jnp.where(n < MAX_EXACT, n, big)
    far = tab_ref[NUM_BUCKETS - 1, h]
    out = jnp.zeros(n.shape, F32)
    for b in range(NUM_BUCKETS - 1):
        out = jnp.where(bucket == b, tab_ref[b, h] - far, out)
    return jnp.where(valid, out, NEG)


def _cmp_block_of(u, n_cmp):
    half = n_cmp // 2
    return 2 * jnp.where(u >= half, u - half, u) + (u >= half).astype(I32)


def _tiles_kernel(tab_ref, diag_ref, prev_ref, slast_ref, snew_ref, swin_ref, scmp_ref, *, n_cmp):
    h = pl.program_id(0)
    kr = _iota2((TQ, TQ), 0)
    qc = _iota2((TQ, TQ), 1)
    diag_ref[0] = _bias_tile(qc - kr, qc >= kr, tab_ref, h)
    prev_ref[0] = _bias_tile(TQ + qc - kr, qc >= -1, tab_ref, h)
    r8 = _iota2((DS_PAD, PAGE), 0)
    c8 = _iota2((DS_PAD, PAGE), 1)
    slast_ref[0] = _bias_tile(PAGE + r8 - c8, r8 >= -1, tab_ref, h)
    snew_ref[0] = _bias_tile(r8 - c8, r8 >= c8, tab_ref, h)
    rw = _iota2((DS_PAD, WINDOW), 0)
    cw = _iota2((DS_PAD, WINDOW), 1)
    swin_ref[0] = _bias_tile(WINDOW + rw - cw, cw > rw, tab_ref, h)
    rc = _iota2((DS_PAD, n_cmp), 0)
    jc = _cmp_block_of(_iota2((DS_PAD, n_cmp), 1), n_cmp)
    scmp_ref[0] = _bias_tile(n_cmp * CMP_BLOCK + rc - (jc * CMP_BLOCK + CMP_BLOCK - 1), rc >= -1, tab_ref, h)


def _bias_tiles(rel_bias, n_cmp_sample):
    nh = rel_bias.shape[1]
    per_head = lambda *s: pl.BlockSpec((1,) + s, lambda h: (h,) + (0,) * len(s))
    shapes = [(TQ, TQ), (TQ, TQ), (DS_PAD, PAGE), (DS_PAD, PAGE), (DS_PAD, WINDOW), (DS_PAD, n_cmp_sample)]
    return pl.pallas_call(
        functools.partial(_tiles_kernel, n_cmp=n_cmp_sample),
        grid=(nh,),
        in_specs=[pl.BlockSpec(memory_space=pltpu.SMEM)],
        out_specs=[per_head(*s) for s in shapes],
        out_shape=[jax.ShapeDtypeStruct((nh,) + s, F32) for s in shapes],
        compiler_params=_cparams("arbitrary"),
        name="bias_tiles",
    )(rel_bias)


def _cmp_tiles_kernel(tab_ref, o_ref, *, head0, n_cmp):
    h = pl.program_id(0) + head0
    i = pl.program_id(1)
    j = _cmp_block_of(_iota2((n_cmp, TQ), 0), n_cmp)
    dist = i * TQ + _iota2((n_cmp, TQ), 1) - (j * CMP_BLOCK + CMP_BLOCK - 1)
    o_ref[0, 0] = _bias_tile(dist, dist >= 0, tab_ref, h)


def _cmp_tiles(rel_bias, head0, n_qblk, n_cmp):
    return pl.pallas_call(
        functools.partial(_cmp_tiles_kernel, head0=head0, n_cmp=n_cmp),
        grid=(MIX_HEADS, n_qblk),
        in_specs=[pl.BlockSpec(memory_space=pltpu.SMEM)],
        out_specs=pl.BlockSpec((1, 1, n_cmp, TQ), lambda h, i: (i, h, 0, 0)),
        out_shape=jax.ShapeDtypeStruct((n_qblk, MIX_HEADS, n_cmp, TQ), F32),
        compiler_params=_cparams("arbitrary", "arbitrary"),
        name="cmp_tiles",
    )(rel_bias)


def _sortable(x):
    b = lax.bitcast_convert_type(x, I32)
    return jnp.where(x == 0.0, 0, b ^ ((b >> 31) & 0x7FFFFFFF))


def _kth_largest(count_ge, shape, k):
    zero = jnp.zeros(shape, I32)
    t0 = jnp.where(count_ge(zero) >= k, zero, jnp.full(shape, INT_MIN, I32))

    def bit_body(bi, t):
        cand = t + jnp.left_shift(jnp.int32(1), 30 - bi)
        return jnp.where(count_ge(cand) >= k, cand, t)

    return lax.fori_loop(0, 31, bit_body, t0)


def _fold_rows(x):
    return x.reshape(x.shape[0] // SUBLANES, SUBLANES, x.shape[1]).sum(axis=0)


def _fold_lanes(x):
    out = x[:, :LANES]
    for j in range(1, x.shape[1] // LANES):
        out = out + x[:, j * LANES:(j + 1) * LANES]
    return out


def _rep_lanes(x, width):
    n = width // LANES
    return x if n == 1 else jnp.concatenate([x] * n, axis=1)


def _stack_heads(src, dst_ref, heads, rows):
    for h in range(heads):
        dst_ref[h * rows:(h + 1) * rows, :] = src[:, h * LANES:(h + 1) * LANES].astype(dst_ref.dtype)


def _tri(n, lower):
    r = np.arange(n)
    m = (r[None, :] < r[:, None]) if lower else (r[:, None] < r[None, :])
    return jnp.asarray(m.astype(np.float32), BF16)


def _select_bias_t(keys, thr, need, run_ref, ltri):
    eq = keys == thr
    eq_f = jnp.where(eq, 1.0, 0.0)
    before = _dot(ltri, eq_f.astype(BF16)) + run_ref[...]
    tie_ok = before < need
    run_ref[...] += jnp.sum(eq_f, axis=0, keepdims=True)
    return jnp.where(keys > thr, 0.0, jnp.where(eq, jnp.where(tie_ok, 0.0, NEG), NEG))


def _flash_init_t(m_ref, l_ref, acc_ref):
    m_ref[...] = jnp.full(m_ref.shape, -3e38, F32)
    l_ref[...] = jnp.zeros_like(l_ref)
    acc_ref[...] = jnp.zeros_like(acc_ref)


def _flash_update_t(lg, vt, m_ref, l_ref, acc_ref):
    m_old = m_ref[...]
    m_new = jnp.maximum(m_old, jnp.max(lg, axis=0, keepdims=True))
    alpha = jnp.exp(m_old - m_new)
    p = jnp.exp(lg - m_new)
    l_ref[...] = alpha * l_ref[...] + jnp.sum(p, axis=0, keepdims=True)
    acc_ref[...] = alpha * acc_ref[...] + _dot(vt, p.astype(BF16))
    m_ref[...] = m_new


def _head_logits(kc, qs_ref, heads, add_of):
    lg = _dot_nt(kc, qs_ref[...])
    return jnp.concatenate([lg[:, r * TQ:(r + 1) * TQ] + add_of(r) for r in range(heads)], axis=1)


def _key_rows(ref, c):
    return ref[0, pl.ds(pl.multiple_of(c * TQ, TQ), TQ), :]


def _dsa_prompt_kernel(qa_ref, iqa_ref, iw_ref, ika_ref, k_ref, vt_ref, tdiag_ref, tprev_ref, ltri_ref,
                       o_ref, keys_ref, qs_ref, iqs_ref, wt_ref, run_ref, m_ref, l_ref, acc_ref, *, k_top):
    i = pl.program_id(1)
    nh = MIX_HEADS
    _stack_heads(qa_ref[0], qs_ref, nh, TQ)
    _stack_heads(iqa_ref[0], iqs_ref, IDX_HEADS, TQ)
    wt_ref[...] = iw_ref[0].T

    def score_chunk(c, causal):
        s = _dot_nt(_key_rows(ika_ref, c), iqs_ref[...])
        sc = jnp.zeros((TQ, TQ), F32)
        for h in range(IDX_HEADS):
            sc = sc + jnp.maximum(s[:, h * TQ:(h + 1) * TQ], 0.0) * wt_ref[MISC_IW + h:MISC_IW + h + 1, :]
        if causal:
            sc = jnp.where(_iota2((TQ, TQ), 0) <= _iota2((TQ, TQ), 1), sc, NEG)
        keys_ref[c] = _sortable(sc)

    def score_body(c, carry):
        score_chunk(c, False)
        return carry

    lax.fori_loop(0, i, score_body, 0)
    score_chunk(i, True)

    def count_ge(cand):
        def body(c, acc):
            return acc + _fold_rows((keys_ref[c] >= cand).astype(I32))

        acc = lax.fori_loop(0, i + 1, body, jnp.zeros((SUBLANES, TQ), I32))
        return jnp.sum(acc, axis=0, keepdims=True)

    thr = _kth_largest(count_ge, (1, TQ), k_top)
    need = (k_top - count_ge(thr + 1)).astype(F32)

    _flash_init_t(m_ref, l_ref, acc_ref)
    run_ref[...] = jnp.zeros_like(run_ref)

    def attend(c, tile_ref):
        sel = _select_bias_t(keys_ref[c], thr, need, run_ref, ltri_ref[...])
        add_of = (lambda h: sel) if tile_ref is None else (lambda h: sel + tile_ref[h])
        lg = _head_logits(_key_rows(k_ref, c), qs_ref, nh, add_of)
        _flash_update_t(lg, vt_ref[0, c], m_ref, l_ref, acc_ref)

    def far_body(c, carry):
        attend(c, None)
        return carry

    lax.fori_loop(0, jnp.maximum(i - 1, 0), far_body, 0)

    @pl.when(i >= 1)
    def _():
        attend(i - 1, tprev_ref)

    attend(i, tdiag_ref)
    for h in range(nh):
        cs = slice(h * TQ, (h + 1) * TQ)
        ot = acc_ref[:, cs] * (1.0 / l_ref[:, cs])
        o_ref[0, :, h * LANES:(h + 1) * LANES] = ot.T.astype(o_ref.dtype)


def _dsa_prompt(m, tdiag, tprev, b, s):
    nq = s // TQ
    k_top = min(IDX_TOPK, s // 4)
    assert TQ >= k_top
    qblk = lambda width: pl.BlockSpec((1, TQ, width), lambda bi, i: (bi, i, 0))
    seq = lambda width: pl.BlockSpec((1, s, width), lambda bi, i: (bi, 0, 0))
    r3 = lambda a: a.reshape(b, s, a.shape[-1])
    cols = MIX_HEADS * TQ
    return pl.pallas_call(
        functools.partial(_dsa_prompt_kernel, k_top=k_top),
        grid=(b, nq),
        in_specs=[qblk(1024), qblk(1024), qblk(LANES), seq(LANES), seq(LANES),
                  pl.BlockSpec((1, nq, LANES, TQ), lambda bi, i: (bi, 0, 0, 0)),
                  _full((MIX_HEADS, TQ, TQ)), _full((MIX_HEADS, TQ, TQ)), _full((TQ, TQ))],
        out_specs=qblk(1024),
        out_shape=jax.ShapeDtypeStruct((b, s, MIX_HEADS * LANES), BF16),
        scratch_shapes=[pltpu.VMEM((nq, TQ, TQ), I32),
                        pltpu.VMEM((cols, LANES), BF16),
                        pltpu.VMEM((IDX_HEADS * TQ, LANES), BF16),
                        pltpu.VMEM((LANES, TQ), F32),
                        pltpu.VMEM((1, TQ), F32),
                        pltpu.VMEM((1, cols), F32),
                        pltpu.VMEM((1, cols), F32),
                        pltpu.VMEM((LANES, cols), F32)],
        compiler_params=_cparams("arbitrary", "arbitrary"),
        name="dsa_prompt",
    )(r3(m["qd"]), r3(m["iq"]), r3(m["iw"]), r3(m["ika"]), r3(m["dk"]), m["dvt"],
      tdiag, tprev, _tri(TQ, True))


def _compress_weights(w_phi, g_ck):
    eye = jnp.eye(N_GROUPS, dtype=F32)
    wbig = jnp.einsum("clde,cx,gy->lcgdxye", w_phi, jnp.eye(2, dtype=F32), eye)
    wbig = wbig.reshape(CMP_BLOCK * 4 * HEAD_DIM, 4 * HEAD_DIM).astype(BF16)
    return wbig, jnp.tile(g_ck, 2)[None, :]


def _compress_kernel(x_ref, wbig_ref, gck_ref, s64_ref, ck_ref, cvt_ref):
    kdim = wbig_ref.shape[0]
    half = x_ref.shape[1]
    for p in range(2):
        c = _dot(x_ref[0, :, p * kdim:(p + 1) * kdim].astype(BF16), wbig_ref[...])
        ck_ref[0, p * half:(p + 1) * half, :] = _norm64(c[:, :LANES], gck_ref[...], s64_ref[...]).astype(BF16)
        cvt_ref[0, :, p * half:(p + 1) * half] = c[:, LANES:].T.astype(BF16)


def _compress_prompt(ckv, cw, b, s):
    wbig, gck = cw
    nc = s // CMP_BLOCK
    kdim = wbig.shape[0]
    x = ckv.reshape(b, nc // 2, 2 * kdim)
    return pl.pallas_call(
        _compress_kernel,
        grid=(b,),
        in_specs=[pl.BlockSpec((1, nc // 2, 2 * kdim), lambda bi: (bi, 0, 0)), _full(wbig.shape),
                  _full((1, LANES)), _full((LANES, LANES))],
        out_specs=[pl.BlockSpec((1, nc, LANES), lambda bi: (bi, 0, 0)),
                   pl.BlockSpec((1, LANES, nc), lambda bi: (bi, 0, 0))],
        out_shape=[jax.ShapeDtypeStruct((b, nc, LANES), BF16), jax.ShapeDtypeStruct((b, LANES, nc), BF16)],
        compiler_params=_cparams("arbitrary"),
        name="compress_prompt",
    )(x, wbig, gck, _seg_matrix())


def _pick_blocks(score, n_sel, axis):
    pos = _iota2(score.shape, axis)
    picked = jnp.zeros(score.shape, F32)
    for _ in range(n_sel):
        top = jnp.max(score, axis=axis, keepdims=True)
        first = jnp.min(jnp.where(score == top, pos, score.shape[axis]), axis=axis, keepdims=True)
        hit = pos == first
        picked = jnp.where(hit, 1.0, picked)
        score = jnp.where(hit, -jnp.inf, score)
    return picked


def _softmax_valid(lg, axis):
    valid = lg > 0.5 * NEG
    p = jnp.where(valid, jnp.exp(lg - jnp.max(lg, axis=axis, keepdims=True)), 0.0)
    tot = jnp.sum(p, axis=axis, keepdims=True)
    return p * (1.0 / jnp.where(tot > 0.0, tot, 1.0))


def _nsa_prompt_kernel(qa_ref, ng_ref, ck_ref, cvt_ref, ks_ref, vst_ref, kw_ref, vwt_ref,
                       tcmp_ref, tdiag_ref, tprev_ref, tw2_ref, exp_ref,
                       o_ref, qs_ref, ngt_ref, oc_ref, os_ref, m_ref, l_ref, acc_ref, *, n_sel):
    i = pl.program_id(1)
    gh = GROUP_HEADS
    n_cmp = ck_ref.shape[1]
    n_blk = n_cmp // 2
    ngt_ref[...] = ng_ref[0].T
    blk = _iota2((n_blk, TQ), 0)
    cur = (i * TQ + _iota2((n_blk, TQ), 1)) // SEL_BLOCK
    forced = (blk == 0) | (blk == cur) | (blk == cur - 1)

    for g in range(N_GROUPS):
        _stack_heads(qa_ref[0, :, g * gh * LANES:(g + 1) * gh * LANES], qs_ref, gh, TQ)
        heads = [(r, g * gh + r, slice(r * TQ, (r + 1) * TQ)) for r in range(gh)]

        pc = _softmax_valid(_head_logits(ck_ref[0], qs_ref, gh, lambda r: tcmp_ref[0, g * gh + r]), 0)
        oc_ref[...] = _dot(cvt_ref[0], pc.astype(BF16))
        imp = pc[:, :TQ]
        for r in range(1, gh):
            imp = imp + pc[:, r * TQ:(r + 1) * TQ]
        imp = imp[:n_blk] + imp[n_blk:]
        score = jnp.where(forced, BIG, jnp.where(blk <= cur, imp, NEG))
        picked = _pick_blocks(score, n_sel, 0).astype(BF16)

        def branch_chunk(k_ref, vt_ref, c, tile_ref, extra):
            if tile_ref is None:
                add_of = lambda r: extra
            elif extra is None:
                add_of = lambda r: tile_ref[g * gh + r]
            else:
                add_of = lambda r: tile_ref[g * gh + r] + extra
            lg = _head_logits(_key_rows(k_ref, c), qs_ref, gh, add_of)
            _flash_update_t(lg, vt_ref[0, c], m_ref, l_ref, acc_ref)

        _flash_init_t(m_ref, l_ref, acc_ref)

        def sel_chunk(c, tile_ref):
            keep = _dot(exp_ref[c], picked)
            branch_chunk(ks_ref, vst_ref, c, tile_ref, jnp.where(keep > 0.5, 0.0, NEG))

        def far_body(c, carry):
            sel_chunk(c, None)
            return carry

        lax.fori_loop(0, jnp.maximum(i - 1, 0), far_body, 0)

        @pl.when(i >= 1)
        def _():
            sel_chunk(i - 1, tprev_ref)

        sel_chunk(i, tdiag_ref)
        os_ref[...] = acc_ref[...] * (1.0 / l_ref[...])

        _flash_init_t(m_ref, l_ref, acc_ref)

        @pl.when(i >= 2)
        def _():
            branch_chunk(kw_ref, vwt_ref, i - 2, None, tw2_ref[...])

        @pl.when(i >= 1)
        def _():
            branch_chunk(kw_ref, vwt_ref, i - 1, tprev_ref, None)

        branch_chunk(kw_ref, vwt_ref, i, tdiag_ref, None)
        inv = 1.0 / l_ref[...]
        for r, h, cs in heads:
            gate = lambda j: ngt_ref[MISC_NG + 3 * h + j:MISC_NG + 3 * h + j + 1, :]
            ot = gate(0) * oc_ref[:, cs] + gate(1) * os_ref[:, cs] + gate(2) * (acc_ref[:, cs] * inv[:, cs])
            o_ref[0, :, h * LANES:(h + 1) * LANES] = ot.T.astype(o_ref.dtype)


def _block_expander_t(n_blk, n_keys):
    key_blk = np.arange(n_keys) // SEL_BLOCK
    e = (key_blk[:, None] == np.arange(n_blk)[None, :]).astype(np.float32)
    return jnp.asarray(e.reshape(n_keys // TQ, TQ, n_blk), BF16)


def _window_edge_tile_t():
    r = np.arange(TQ)
    return jnp.asarray(np.where(r[:, None] > r[None, :], 0.0, NEG).astype(np.float32))


def _nsa_prompt(m, ck, cvt, tcmp, tdiag, tprev, b, s):
    nq = s // TQ
    n_cmp = s // CMP_BLOCK
    n_blk = s // SEL_BLOCK
    n_sel = min(N_SEL, n_blk)
    assert WINDOW == 2 * TQ
    qblk = lambda width: pl.BlockSpec((1, TQ, width), lambda bi, i: (bi, i, 0))
    seq = lambda rows, width: pl.BlockSpec((1, rows, width), lambda bi, i: (bi, 0, 0))
    seqt = pl.BlockSpec((1, nq, LANES, TQ), lambda bi, i: (bi, 0, 0, 0))
    r3 = lambda a: a.reshape(b, s, a.shape[-1])
    cols = GROUP_HEADS * TQ
    return pl.pallas_call(
        functools.partial(_nsa_prompt_kernel, n_sel=n_sel),
        grid=(b, nq),
        in_specs=[qblk(1024), qblk(LANES), seq(n_cmp, LANES), seq(LANES, n_cmp),
                  seq(s, LANES), seqt, seq(s, LANES), seqt,
                  pl.BlockSpec((1, MIX_HEADS, n_cmp, TQ), lambda bi, i: (i, 0, 0, 0)),
                  _full((MIX_HEADS, TQ, TQ)), _full((MIX_HEADS, TQ, TQ)), _full((TQ, TQ)),
                  _full((nq, TQ, n_blk))],
        out_specs=qblk(1024),
        out_shape=jax.ShapeDtypeStruct((b, s, MIX_HEADS * LANES), BF16),
        scratch_shapes=[pltpu.VMEM((cols, LANES), BF16),
                        pltpu.VMEM((LANES, TQ), F32),
                        pltpu.VMEM((LANES, cols), F32),
                        pltpu.VMEM((LANES, cols), F32),
                        pltpu.VMEM((1, cols), F32),
                        pltpu.VMEM((1, cols), F32),
                        pltpu.VMEM((LANES, cols), F32)],
        compiler_params=_cparams("arbitrary", "arbitrary"),
        name="nsa_prompt",
    )(r3(m["qn"]), r3(m["ng"]), ck, cvt, r3(m["sk"]), m["svt"], r3(m["wk"]), m["wvt"],
      tcmp, tdiag, tprev, _window_edge_tile_t(), _block_expander_t(n_blk, s))


def _merge_kernel(h_ref, oa_ref, ob_ref, ga_ref, gb_ref, wa_ref, wb_ref, wo_ref, o_ref):
    merged = ga_ref[...] * _dot(oa_ref[...], wa_ref[...]) + gb_ref[...] * _dot(ob_ref[...], wb_ref[...])
    o_ref[...] = h_ref[...] + _dot(merged.astype(BF16), wo_ref[...])


def _pad_head_rows(w):
    return _pad_heads(w.T, True).T.astype(BF16)


def _merge(h, oa, ob, ga, gb, wa, wb, wo, tm):
    n, d = h.shape
    row = lambda width: pl.BlockSpec((tm, width), lambda i: (i, 0))
    return pl.pallas_call(
        _merge_kernel,
        grid=(n // tm,),
        in_specs=[row(d), row(1024), row(1024), row(d), row(d),
                  _full(wa.shape), _full(wb.shape), _full(wo.shape)],
        out_specs=row(d),
        out_shape=jax.ShapeDtypeStruct((n, d), F32),
        compiler_params=_cparams("arbitrary"),
        name="merge",
    )(h, oa, ob, ga, gb, wa, wb, wo)


STEP_KEYS = PAGES_PER_STEP * PAGE


def _page_specs(shape_tail, step_of):
    nd = len(shape_tail)

    def spec(pp):
        def imap(b, s, pt):
            return (pt[b, step_of(s) * PAGES_PER_STEP + pp],) + (0,) * nd
        return pl.BlockSpec((1,) + shape_tail, imap)

    return [spec(pp) for pp in range(PAGES_PER_STEP)]


def _feature_major_pages(cache):
    n, rows = cache.shape[:2]
    nd = cache.ndim
    return cache.transpose((0,) + tuple(range(2, nd)) + (1,)).reshape(n, -1, rows)


def _cat_pages(refs, lo, hi):
    return jnp.concatenate([r[0, lo:hi, :].astype(BF16) for r in refs], axis=1)


def _pad_rows(x, pad_ref, transpose=False):
    pad_ref[...] = jnp.zeros_like(pad_ref)
    pad_ref[0:DS_PAD, :] = x
    full = pad_ref[...]
    return (full.T if transpose else full).astype(BF16)


def _select_bias_rows(keys, thr, need, run_ref, utri, ones):
    out = []
    for j in range(keys.shape[1] // PAGE):
        kj = keys[:, j * PAGE:(j + 1) * PAGE]
        eq = kj == thr
        eq_b = jnp.where(eq, 1.0, 0.0).astype(BF16)
        tie_ok = _dot(eq_b, utri) + run_ref[...] < need
        run_ref[...] += _dot(eq_b, ones)
        out.append(jnp.where(kj > thr, 0.0, jnp.where(eq, jnp.where(tie_ok, 0.0, NEG), NEG)))
    return out[0] if len(out) == 1 else jnp.concatenate(out, axis=1)


def _flash_init(m_ref, l_ref, acc_ref):
    m_ref[...] = jnp.full(m_ref.shape, -3e38, F32)
    l_ref[...] = jnp.zeros_like(l_ref)
    acc_ref[...] = jnp.zeros_like(acc_ref)


def _flash_update_rows(lg, vt, m_ref, l_ref, acc_ref):
    m_old = m_ref[...]
    m_new = jnp.maximum(m_old, jnp.max(lg, axis=-1, keepdims=True))
    alpha = jnp.exp(m_old - m_new)
    p = jnp.exp(lg - m_new)
    l_ref[...] = alpha * l_ref[...] + jnp.sum(p, axis=-1, keepdims=True)
    acc_ref[...] = alpha * acc_ref[...] + _dot_nt(p.astype(BF16), vt)
    m_ref[...] = m_new


def _last_page_tile(tile, on):
    t = jnp.where(on, tile, 0.0)
    if PAGES_PER_STEP == 1:
        return t
    return jnp.concatenate([jnp.zeros((DS_PAD, STEP_KEYS - PAGE), F32), t], axis=1)


def _dsa_sample_kernel(pt_ref, qa_ref, iqa_ref, iw_ref, iknew_ref, kvnew_ref, *rest, n_pages, k_top):
    npp = PAGES_PER_STEP
    idx_refs, kv_refs = rest[:npp], rest[npp:2 * npp]
    (slast_ref, snew_ref, utri_ref, ones_ref, o_ref, keys_ref, keysn_ref, qs_ref, iqs_ref, wb_ref,
     thr_ref, need_ref, run_ref, pad_ref, m_ref, l_ref, acc_ref) = rest[2 * npp:]
    s = pl.program_id(1)
    ns = n_pages // npp
    nh = MIX_HEADS
    rq = DS_PAD

    def score(sd):
        sc = jnp.zeros((rq, sd.shape[1]), F32)
        for h in range(IDX_HEADS):
            sc = sc + jnp.maximum(sd[h * rq:(h + 1) * rq], 0.0) * _rep_lanes(wb_ref[h], sd.shape[1])
        return sc

    @pl.when(s == 0)
    def _():
        _stack_heads(qa_ref[...], qs_ref, nh, rq)
        _stack_heads(iqa_ref[...], iqs_ref, IDX_HEADS, rq)
        iw = iw_ref[...]
        for h in range(IDX_HEADS):
            wb_ref[h] = jnp.broadcast_to(iw[:, MISC_IW + h:MISC_IW + h + 1], (rq, LANES))

    @pl.when(s < ns)
    def _():
        ikt = _cat_pages(idx_refs, 0, HEAD_DIM)
        keys_ref[s] = _sortable(score(_dot(iqs_ref[:, :HEAD_DIM].astype(BF16), ikt)))

    @pl.when(s == ns - 1)
    def _():
        sc = score(_dot_nt(iqs_ref[...].astype(BF16), _pad_rows(iknew_ref[...], pad_ref)))
        sc = jnp.where(_iota2((rq, PAGE), 0) >= _iota2((rq, PAGE), 1), sc, NEG)
        keysn_ref[...] = _sortable(sc)

        def count_ge(cand):
            cw = _rep_lanes(cand, STEP_KEYS)

            def body(c, acc):
                return acc + _fold_lanes((keys_ref[c] >= cw).astype(I32))

            acc = lax.fori_loop(0, ns, body, (keysn_ref[...] >= cand).astype(I32))
            return jnp.broadcast_to(jnp.sum(acc, axis=-1, keepdims=True), acc.shape)

        thr = _kth_largest(count_ge, (rq, LANES), k_top)
        thr_ref[...] = thr
        need_ref[...] = (k_top - count_ge(thr + 1)).astype(F32)
        _flash_init(m_ref, l_ref, acc_ref)
        run_ref[...] = jnp.zeros_like(run_ref)

    def attend(keys, kt, vt, tile_of):
        sel = _select_bias_rows(keys, thr_ref[...], need_ref[...], run_ref, utri_ref[...], ones_ref[...])
        lg = _dot(qs_ref[...].astype(BF16), kt)
        parts = [lg[h * rq:(h + 1) * rq] + sel + tile_of(h) for h in range(nh)]
        _flash_update_rows(jnp.concatenate(parts, axis=0), vt, m_ref, l_ref, acc_ref)

    @pl.when(s >= ns)
    def _():
        is_last = s == 2 * ns - 1
        attend(keys_ref[s - ns], _cat_pages(kv_refs, 0, LANES), _cat_pages(kv_refs, LANES, 2 * LANES),
               lambda h: _last_page_tile(slast_ref[h], is_last))

    @pl.when(s == 2 * ns - 1)
    def _():
        kvn = kvnew_ref[...]
        ktn = _pad_rows(kvn[:, :LANES], pad_ref, True)
        vtn = _pad_rows(kvn[:, LANES:], pad_ref, True)
        attend(keysn_ref[...], ktn, vtn, lambda h: snew_ref[h])
        inv = 1.0 / l_ref[...]
        for h in range(nh):
            rows = slice(h * rq, (h + 1) * rq)
            o_ref[:, h * LANES:(h + 1) * LANES] = acc_ref[rows, :] * inv[rows]


def _dsa_sample(m, kvnew, cache_idx, cache_kv, page_table, slast, snew, db):
    n_pages = page_table.shape[1]
    ns = n_pages // PAGES_PER_STEP
    assert ns * PAGES_PER_STEP == n_pages
    k_top = min(IDX_TOPK, (n_pages * PAGE + 4) // 4)
    row = lambda width: pl.BlockSpec((DS_PAD, width), lambda b, s, pt: (b, 0))
    const = lambda shape: pl.BlockSpec(shape, lambda b, s, pt: (0,) * len(shape))
    rows = MIX_HEADS * DS_PAD
    grid_spec = pltpu.PrefetchScalarGridSpec(
        num_scalar_prefetch=1,
        grid=(db, 2 * ns),
        in_specs=[row(1024), row(1024), row(LANES), row(LANES), row(2 * LANES)]
        + _page_specs((HEAD_DIM, PAGE), lambda s: jnp.minimum(s, ns - 1))
        + _page_specs((2 * LANES, PAGE), lambda s: jnp.maximum(s - ns, 0))
        + [const((MIX_HEADS, DS_PAD, PAGE)), const((MIX_HEADS, DS_PAD, PAGE)), const((PAGE, PAGE)),
           const((PAGE, PAGE))],
        out_specs=row(1024),
        scratch_shapes=[pltpu.VMEM((ns, DS_PAD, STEP_KEYS), I32), pltpu.VMEM((DS_PAD, PAGE), I32),
                        pltpu.VMEM((rows, LANES), F32), pltpu.VMEM((rows, LANES), F32),
                        pltpu.VMEM((IDX_HEADS, DS_PAD, LANES), F32),
                        pltpu.VMEM((DS_PAD, LANES), I32), pltpu.VMEM((DS_PAD, LANES), F32),
                        pltpu.VMEM((DS_PAD, LANES), F32), pltpu.VMEM((PAGE, LANES), F32),
                        pltpu.VMEM((rows, 1), F32), pltpu.VMEM((rows, 1), F32),
                        pltpu.VMEM((rows, LANES), F32)])
    idx_pages = _feature_major_pages(cache_idx)
    kv_pages = _feature_major_pages(cache_kv)
    return pl.pallas_call(
        functools.partial(_dsa_sample_kernel, n_pages=n_pages, k_top=k_top),
        grid_spec=grid_spec,
        out_shape=jax.ShapeDtypeStruct((db * DS_PAD, 1024), F32),
        compiler_params=_cparams("arbitrary", "arbitrary"),
        name="dsa_sample",
    )(page_table, m["qd"], m["iq"], m["iw"], m["ika"].astype(F32), kvnew,
      *([idx_pages] * PAGES_PER_STEP), *([kv_pages] * PAGES_PER_STEP), slast, snew,
      _tri(PAGE, False), jnp.ones((PAGE, PAGE), BF16))


def _compress_sample_kernel(pt_ref, *rest, n_pages):
    npp = PAGES_PER_STEP
    page_refs = rest[:npp]
    wbig_ref, gck_ref, s64_ref, ck_ref, cv_ref, x_ref = rest[npp:]
    s = pl.program_id(1)
    for pp in range(npp):
        r0 = pl.multiple_of((s * npp + pp) * PAGE, PAGE)
        for j in range(2):
            x_ref[j, pl.ds(r0, PAGE), :] = page_refs[pp][0, j * LANES:(j + 1) * LANES, :].T

    @pl.when(s == n_pages // npp - 1)
    def _():
        half = n_pages * PAGE // (2 * CMP_BLOCK)
        width = 2 * LANES
        for p in range(2):
            acc = jnp.zeros((half, width), F32)
            for l in range(CMP_BLOCK):
                rows = pl.ds(p * CMP_BLOCK + l, half, stride=2 * CMP_BLOCK)
                xl = jnp.concatenate([x_ref[0, rows, :], x_ref[1, rows, :]], axis=1).astype(BF16)
                acc = acc + _dot(xl, wbig_ref[l * width:(l + 1) * width, :])
            ck_ref[0, p * half:(p + 1) * half, :] = _norm64(acc[:, :LANES], gck_ref[...], s64_ref[...]).astype(BF16)
            cv_ref[0, p * half:(p + 1) * half, :] = acc[:, LANES:].astype(BF16)


def _compress_sample(cache_cmp, page_table, cw, db):
    wbig, gck = cw
    n_pages = page_table.shape[1]
    n_cmp = n_pages * PAGE // CMP_BLOCK
    const = lambda shape: pl.BlockSpec(shape, lambda b, s, pt: (0,) * len(shape))
    out = jax.ShapeDtypeStruct((db, n_cmp, LANES), BF16)
    grid_spec = pltpu.PrefetchScalarGridSpec(
        num_scalar_prefetch=1,
        grid=(db, n_pages // PAGES_PER_STEP),
        in_specs=_page_specs((2 * LANES, PAGE), lambda s: s)
        + [const(wbig.shape), const((1, LANES)), const((LANES, LANES))],
        out_specs=[pl.BlockSpec((1, n_cmp, LANES), lambda b, s, pt: (b, 0, 0))] * 2,
        scratch_shapes=[pltpu.VMEM((2, n_pages * PAGE, LANES), F32)])
    return pl.pallas_call(
        functools.partial(_compress_sample_kernel, n_pages=n_pages),
        grid_spec=grid_spec,
        out_shape=[out, out],
        compiler_params=_cparams("arbitrary", "arbitrary"),
        name="compress_sample",
    )(page_table, *([_feature_major_pages(cache_cmp)] * PAGES_PER_STEP), wbig, gck, _seg_matrix())


def _nsa_sample_kernel(pt_ref, qa_ref, ng_ref, ck_ref, cv_ref, win_ref, sknew_ref, wknew_ref, *rest,
                       n_pages, n_sel):
    npp = PAGES_PER_STEP
    sel_refs = rest[:npp]
    (scmp_ref, slast_ref, snew_ref, swin_ref, exp_ref, o_ref,
     qs_ref, picked_ref, oc_ref, pad_ref, m_ref, l_ref, acc_ref) = rest[npp:]
    s = pl.program_id(1)
    ns = n_pages // npp
    nh = MIX_HEADS
    gh = GROUP_HEADS
    rq = DS_PAD
    n_cmp = ck_ref.shape[1]
    n_past_blk = n_cmp // 2
    nb = picked_ref.shape[2]

    def tiled(lg, tile_of, extra=None):
        parts = []
        for h in range(nh):
            part = lg[h * rq:(h + 1) * rq] + tile_of(h)
            if extra is not None:
                part = part + extra[h // gh]
            parts.append(part)
        return jnp.concatenate(parts, axis=0)

    @pl.when(s == 0)
    def _():
        _stack_heads(qa_ref[...], qs_ref, nh, rq)
        lc = tiled(_dot_nt(qs_ref[...].astype(BF16), ck_ref[0]), lambda h: scmp_ref[h])
        pc = _softmax_valid(lc, 1)
        oc_ref[...] = _dot(pc.astype(BF16), cv_ref[0])
        blk = _iota2((rq, nb), 1)
        forced = (blk == 0) | (blk == n_past_blk) | (blk == n_past_blk - 1)
        for g in range(N_GROUPS):
            imp = pc[g * gh * rq:(g * gh + 1) * rq]
            for r in range(1, gh):
                imp = imp + pc[(g * gh + r) * rq:(g * gh + r + 1) * rq]
            imp = imp[:, :n_past_blk] + imp[:, n_past_blk:]
            imp = jnp.concatenate([imp, jnp.zeros((rq, nb - n_past_blk), F32)], axis=1)
            score = jnp.where(forced, BIG, jnp.where(blk < n_past_blk, imp, -jnp.inf))
            picked_ref[g] = _pick_blocks(score, n_sel, 1)
        _flash_init(m_ref, l_ref, acc_ref)

    keep = [jnp.where(_dot(picked_ref[g].astype(BF16), exp_ref[s]) > 0.5, 0.0, NEG) for g in range(N_GROUPS)]
    lg = _dot(qs_ref[...].astype(BF16), _cat_pages(sel_refs, 0, LANES))
    is_last = s == ns - 1
    _flash_update_rows(tiled(lg, lambda h: _last_page_tile(slast_ref[h], is_last), keep),
                       _cat_pages(sel_refs, LANES, 2 * LANES), m_ref, l_ref, acc_ref)

    @pl.when(s == ns - 1)
    def _():
        ng = ng_ref[...]
        new_tile = lambda h: snew_ref[h]
        skn = sknew_ref[...]
        ktn = _pad_rows(skn[:, :LANES], pad_ref, True)
        vtn = _pad_rows(skn[:, LANES:], pad_ref, True)
        _flash_update_rows(tiled(_dot(qs_ref[...].astype(BF16), ktn), new_tile), vtn, m_ref, l_ref, acc_ref)
        o_s = acc_ref[...] * (1.0 / l_ref[...])
        _flash_init(m_ref, l_ref, acc_ref)
        lg = _dot(qs_ref[...].astype(BF16), win_ref[0, :LANES, :].astype(BF16))
        _flash_update_rows(tiled(lg, lambda h: swin_ref[h]), win_ref[0, LANES:, :].astype(BF16),
                           m_ref, l_ref, acc_ref)
        wkn = wknew_ref[...]
        ktn = _pad_rows(wkn[:, :LANES], pad_ref, True)
        vtn = _pad_rows(wkn[:, LANES:], pad_ref, True)
        _flash_update_rows(tiled(_dot(qs_ref[...].astype(BF16), ktn), new_tile), vtn, m_ref, l_ref, acc_ref)
        o_w = acc_ref[...] * (1.0 / l_ref[...])
        o_c = oc_ref[...]
        for h in range(nh):
            sl = slice(h * rq, (h + 1) * rq)
            gate = lambda j: ng[:, MISC_NG + 3 * h + j:MISC_NG + 3 * h + j + 1]
            o_ref[:, h * LANES:(h + 1) * LANES] = gate(0) * o_c[sl] + gate(1) * o_s[sl] + gate(2) * o_w[sl]


def _sample_block_expander(n_pages, nb):
    key_blk = np.arange(n_pages * PAGE) // SEL_BLOCK
    e = (np.arange(nb)[:, None] == key_blk[None, :]).astype(np.float32)
    return jnp.asarray(e.reshape(nb, n_pages // PAGES_PER_STEP, STEP_KEYS).transpose(1, 0, 2), BF16)


def _nsa_sample(m, sknew, wknew, ck, cv, cache_sel, win_t, page_table, scmp, slast, snew, swin, db):
    n_pages = page_table.shape[1]
    ns = n_pages // PAGES_PER_STEP
    n_cmp = ck.shape[1]
    n_sblk = n_cmp // 2 + 1
    n_sel = min(N_SEL, n_sblk)
    nb = -(-n_sblk // LANES) * LANES
    assert win_t.shape[2] == WINDOW
    row = lambda width: pl.BlockSpec((DS_PAD, width), lambda b, s, pt: (b, 0))
    seq = lambda r, width: pl.BlockSpec((1, r, width), lambda b, s, pt: (b, 0, 0))
    const = lambda shape: pl.BlockSpec(shape, lambda b, s, pt: (0,) * len(shape))
    rows = MIX_HEADS * DS_PAD
    grid_spec = pltpu.PrefetchScalarGridSpec(
        num_scalar_prefetch=1,
        grid=(db, ns),
        in_specs=[row(1024), row(LANES), seq(n_cmp, LANES), seq(n_cmp, LANES), seq(2 * LANES, WINDOW),
                  row(2 * LANES), row(2 * LANES)]
        + _page_specs((2 * LANES, PAGE), lambda s: s)
        + [const((MIX_HEADS, DS_PAD, n_cmp)), const((MIX_HEADS, DS_PAD, PAGE)),
           const((MIX_HEADS, DS_PAD, PAGE)), const((MIX_HEADS, DS_PAD, WINDOW)), const((ns, nb, STEP_KEYS))],
        out_specs=row(1024),
        scratch_shapes=[pltpu.VMEM((rows, LANES), F32), pltpu.VMEM((N_GROUPS, DS_PAD, nb), F32),
                        pltpu.VMEM((rows, LANES), F32), pltpu.VMEM((PAGE, LANES), F32),
                        pltpu.VMEM((rows, 1), F32), pltpu.VMEM((rows, 1), F32),
                        pltpu.VMEM((rows, LANES), F32)])
    return pl.pallas_call(
        functools.partial(_nsa_sample_kernel, n_pages=n_pages, n_sel=n_sel),
        grid_spec=grid_spec,
        out_shape=jax.ShapeDtypeStruct((db * DS_PAD, 1024), F32),
        compiler_params=_cparams("arbitrary", "arbitrary"),
        name="nsa_sample",
    )(page_table, m["qn"], m["ng"], ck, cv, win_t, sknew, wknew,
      *([_feature_major_pages(cache_sel)] * PAGES_PER_STEP), scmp, slast, snew, swin,
      _sample_block_expander(n_pages, nb))


def _kv_leaf(t):
    nb, _, s = t.shape
    return t.reshape(nb, 2, N_GROUPS, HEAD_DIM, s).transpose(0, 4, 1, 2, 3)


def kernel(x_prompt, x_sample, cache_dsa_kv, cache_dsa_idx_k, cache_nsa_cmp_kv, cache_nsa_sel_kv,
           state_nsa_win_kv, page_table, rel_bias, ffn1_norm, ffn1_w_gu, ffn1_w_down, mix_norm, w_in,
           dsa_q_norm, dsa_k_norm, idx_k_norm, nsa_q_norm, nsa_cmp_k_norm, nsa_sel_k_norm, nsa_win_k_norm,
           nsa_w_phi, w_dsa_o, w_nsa_o, w_out, ffn2_norm, ffn2_w_gu, ffn2_w_down):
    b, s, d = x_prompt.shape
    db, ds, _ = x_sample.shape
    depth = ffn1_norm.shape[0]
    n_pages = page_table.shape[1]
    past = n_pages * PAGE
    assert s % TQ == 0 and ds <= DS_PAD and ds < CMP_BLOCK and past >= WINDOW
    n_p, n_s = b * s, db * DS_PAD
    tm_p, tm_s = min(512, s), min(512, n_s)

    hp = x_prompt.reshape(n_p, d)
    hs = jnp.pad(x_sample, ((0, 0), (0, DS_PAD - ds), (0, 0))).reshape(n_s, d)
    dsa_h = slice(0, MIX_HEADS)
    nsa_h = slice(MIX_HEADS, 2 * MIX_HEADS)
    tdiag, tprev, slast, snew, swin, scmp = _bias_tiles(rel_bias, past // CMP_BLOCK)
    tcmp = _cmp_tiles(rel_bias, MIX_HEADS, s // TQ, s // CMP_BLOCK)

    outs_p = [[] for _ in range(5)]
    outs_s = [[] for _ in range(5)]
    for l in range(depth):
        pw = _proj_weights(w_in[l], dsa_q_norm[l], dsa_k_norm[l], idx_k_norm[l], nsa_q_norm[l],
                           nsa_sel_k_norm[l], nsa_win_k_norm[l])
        cw = _compress_weights(nsa_w_phi[l], nsa_cmp_k_norm[l])
        wa, wb = _pad_head_rows(w_dsa_o[l]), _pad_head_rows(w_nsa_o[l])
        wo = w_out[l].astype(BF16)

        hp = _ffn(hp, ffn1_norm[l], ffn1_w_gu[l], ffn1_w_down[l], tm_p)
        mp = _proj(hp, mix_norm[l], pw, b, tm_p, BF16, True)
        oa = _dsa_prompt(mp, tdiag[dsa_h], tprev[dsa_h], b, s)
        ck, cvt = _compress_prompt(mp["ckv"], cw, b, s)
        ob = _nsa_prompt(mp, ck, cvt, tcmp, tdiag[nsa_h], tprev[nsa_h], b, s)
        hp = _merge(hp, oa.reshape(n_p, -1), ob.reshape(n_p, -1), mp["ga"], mp["gb"], wa, wb, wo, tm_p)
        hp = _ffn(hp, ffn2_norm[l], ffn2_w_gu[l], ffn2_w_down[l], tm_p)
        win_len = min(WINDOW, s)
        for lst, val in zip(outs_p, (_kv_leaf(mp["dkvt"]), mp["ikt"].transpose(0, 2, 1), _kv_leaf(mp["ckvt"]),
                                     _kv_leaf(mp["skvt"]), _kv_leaf(mp["wkvt"][:, :, s - win_len:]))):
            lst.append(val)

        hs = _ffn(hs, ffn1_norm[l], ffn1_w_gu[l], ffn1_w_down[l], tm_s)
        ms = _proj(hs, mix_norm[l], pw, 1, tm_s, F32, False)
        new_rows = lambda name: ms[name][0].T
        new_t = lambda name: ms[name][0].reshape(-1, db, DS_PAD)[:, :, :ds].transpose(1, 0, 2)
        oa_s = _dsa_sample(ms, new_rows("dkvt"), cache_dsa_idx_k[l], cache_dsa_kv[l], page_table,
                           slast[dsa_h], snew[dsa_h], db)
        ck_s, cv_s = _compress_sample(cache_nsa_cmp_kv[l], page_table, cw, db)
        win_t = _feature_major_pages(state_nsa_win_kv[l])
        ob_s = _nsa_sample(ms, new_rows("skvt"), new_rows("wkvt"), ck_s, cv_s, cache_nsa_sel_kv[l], win_t,
                           page_table, scmp[nsa_h], slast[nsa_h], snew[nsa_h], swin[nsa_h], db)
        hs = _merge(hs, oa_s.astype(BF16), ob_s.astype(BF16), ms["ga"], ms["gb"], wa, wb, wo, tm_s)
        hs = _ffn(hs, ffn2_norm[l], ffn2_w_gu[l], ffn2_w_down[l], tm_s)
        win_new = jnp.concatenate([win_t, new_t("wkvt")], axis=2)
        win_new = win_new[:, :, win_new.shape[2] - min(WINDOW, past + ds):]
        for lst, val in zip(outs_s, (_kv_leaf(new_t("dkvt")), new_t("ikt").transpose(0, 2, 1),
                                     _kv_leaf(new_t("ckvt")), _kv_leaf(new_t("skvt")), _kv_leaf(win_new))):
            lst.append(val)

    y_p = hp.reshape(b, s, d)
    y_s = hs.reshape(db, DS_PAD, d)[:, :ds]
    return (y_p, y_s) + tuple(jnp.stack(o) for o in outs_p) + tuple(jnp.stack(o) for o in outs_s)
```

```python
import functools
import math

import numpy as np
import jax
import jax.numpy as jnp
from jax import lax
from jax.experimental import pallas as pl
from jax.experimental.pallas import tpu as pltpu

F32 = jnp.float32
BF16 = jnp.bfloat16
I32 = jnp.int32

HEAD_DIM = 64
LANES = 128
SUBLANES = 8
N_GROUPS = 2
GROUP_HEADS = 4
MIX_HEADS = N_GROUPS * GROUP_HEADS
IDX_HEADS = 8
IDX_TOPK = 256
CMP_BLOCK = 32
SEL_BLOCK = 64
N_SEL = 16
WINDOW = 512
NUM_BUCKETS = 32
MAX_EXACT = 16
MAX_DISTANCE = 128
PAGE = 128
EPS = 1e-6
NEG = -1e30
BIG = 1e30
TQ = 256
DS_PAD = 8
PAGES_PER_STEP = 16
VMEM_LIMIT = 56 * 1024 * 1024

INT_MIN = -(2 ** 31)
LOG2E = math.log2(math.e)


def _bucket_thresholds():
    n = np.arange(MAX_EXACT, 4 * MAX_DISTANCE)
    ratio = np.log(n.astype(np.float32) / np.float32(MAX_EXACT)) / np.float32(math.log(MAX_DISTANCE / MAX_EXACT))
    big = np.minimum(MAX_EXACT + (ratio * np.float32(NUM_BUCKETS - MAX_EXACT)).astype(np.int32), NUM_BUCKETS - 1)
    return [int(n[np.argmax(big >= b)]) for b in range(MAX_EXACT + 1, NUM_BUCKETS)]


BUCKET_THR = _bucket_thresholds()


def _dot(a, b):
    return jnp.dot(a, b, preferred_element_type=F32)


def _dot_nt(a, b):
    return lax.dot_general(a, b, (((1,), (1,)), ((), ())), preferred_element_type=F32)


def _cparams(*sem):
    return pltpu.CompilerParams(dimension_semantics=sem, vmem_limit_bytes=VMEM_LIMIT)


def _full(shape):
    nd = len(shape)
    return pl.BlockSpec(shape, lambda *_: (0,) * nd)


def _resident(shape):
    nd = len(shape)
    return pl.BlockSpec(shape, lambda *_: (0,) * nd, pipeline_mode=pl.Buffered(1))


def _iota2(shape, axis):
    return lax.broadcasted_iota(I32, shape, axis)


def _rms_rows(x, g):
    return x * lax.rsqrt(jnp.mean(x * x, axis=-1, keepdims=True) + EPS) * g


def _norm64(z, gain, s64):
    outs = []
    for j in range(z.shape[1] // LANES):
        zb = z[:, j * LANES:(j + 1) * LANES]
        sq = zb * zb
        hi = sq.astype(BF16)
        lo = (sq - hi.astype(F32)).astype(BF16)
        ms = (_dot(hi, s64) + _dot(lo, s64)) * (1.0 / HEAD_DIM)
        outs.append(zb * lax.rsqrt(ms + EPS) * gain[:, j * LANES:(j + 1) * LANES])
    return outs[0] if len(outs) == 1 else jnp.concatenate(outs, axis=1)


def _seg_matrix():
    r = np.arange(LANES) // HEAD_DIM
    return jnp.asarray((r[:, None] == r[None, :]).astype(np.float32), BF16)


def _ffn_kernel(x_ref, g_ref, wg_ref, wu_ref, wd_ref, o_ref, xn_ref, acc_ref):
    x = x_ref[...]
    xn_ref[...] = _rms_rows(x, g_ref[...]).astype(BF16)
    acc_ref[...] = jnp.zeros_like(acc_ref)

    def body(c, carry):
        xn = xn_ref[...]
        a = _dot(xn, wg_ref[c])
        b = _dot(xn, wu_ref[c])
        act = (a * jax.nn.sigmoid(a) * b).astype(BF16)
        acc_ref[...] += _dot(act, wd_ref[c])
        return carry

    lax.fori_loop(0, wg_ref.shape[0], body, 0)
    o_ref[...] = x + 0.5 * acc_ref[...]


def _ffn(x, g, w_gu, w_dn, tm):
    n, d = x.shape
    d_ff = w_dn.shape[0]
    fc = 256
    nc = d_ff // fc
    assert nc * fc == d_ff and n % tm == 0
    wg = w_gu[:, :d_ff].reshape(d, nc, fc).transpose(1, 0, 2).astype(BF16)
    wu = w_gu[:, d_ff:].reshape(d, nc, fc).transpose(1, 0, 2).astype(BF16)
    wd = w_dn.reshape(nc, fc, d).astype(BF16)
    return pl.pallas_call(
        _ffn_kernel,
        grid=(n // tm,),
        in_specs=[pl.BlockSpec((tm, d), lambda i: (i, 0)), _full((1, d)),
                  _resident((nc, d, fc)), _resident((nc, d, fc)), _resident((nc, fc, d))],
        out_specs=pl.BlockSpec((tm, d), lambda i: (i, 0)),
        out_shape=jax.ShapeDtypeStruct((n, d), F32),
        scratch_shapes=[pltpu.VMEM((tm, d), BF16), pltpu.VMEM((tm, d), F32)],
        compiler_params=_cparams("arbitrary"),
        name="ffn",
    )(x, g.reshape(1, d), wg, wu, wd)


_W_QD, _W_IQ, _W_QN = 0, 1024, 2048
_W_DKV, _W_CKV, _W_SKV, _W_WKV = 3072, 3328, 3584, 3840
_W_MG, _W_IK, _W_MISC, _W_TOTAL = 4096, 6144, 6272, 6400
MISC_IW = 0
MISC_NG = 8


def _proj_kernel(h_ref, gm_ref, w_ref, gqd_ref, gqn_ref, gk_ref, gik_ref, s64_ref, *out_refs, with_vt):
    refs = list(out_refs)
    hn_ref = refs.pop()
    qd_ref, iq_ref, qn_ref = refs[:3]
    kv_refs = refs[3:3 + (4 if with_vt else 3) * 3]
    ckv_ref, ckvt_ref, ga_ref, gb_ref, ikt_ref, ika_ref, iw_ref, ng_ref = refs[len(refs) - 8:]
    hn_ref[...] = _rms_rows(h_ref[...], gm_ref[...]).astype(BF16)
    s64 = s64_ref[...]
    tm = h_ref.shape[0]

    def seg(a, b):
        return _dot(hn_ref[...], w_ref[:, a:b])

    qd_ref[...] = _norm64(seg(_W_QD, _W_IQ), gqd_ref[...], s64).astype(qd_ref.dtype)
    iq_ref[...] = (seg(_W_IQ, _W_QN) * (HEAD_DIM ** -0.5)).astype(iq_ref.dtype)
    qn_ref[...] = _norm64(seg(_W_QN, _W_DKV), gqn_ref[...], s64).astype(qn_ref.dtype)
    per = 4 if with_vt else 3
    for j, a in enumerate((_W_DKV, _W_SKV, _W_WKV)):
        kvt_ref, k_ref, v_ref = kv_refs[j * per:j * per + 3]
        z = seg(a, a + 2 * LANES)
        kn = _norm64(z[:, :LANES], gk_ref[:, j * LANES:(j + 1) * LANES], s64)
        vt = z[:, LANES:].T
        kvt_ref[0, :LANES, :] = kn.T
        kvt_ref[0, LANES:, :] = vt
        k_ref[...] = kn.astype(BF16)
        v_ref[...] = z[:, LANES:].astype(BF16)
        if with_vt:
            vt_ref = kv_refs[j * per + 3]
            for cc in range(tm // TQ):
                vt_ref[0, cc] = vt[:, cc * TQ:(cc + 1) * TQ].astype(BF16)
    zc = seg(_W_CKV, _W_SKV)
    ckv_ref[...] = zc
    ckvt_ref[0, :LANES, :] = zc[:, :LANES].T
    ckvt_ref[0, LANES:, :] = zc[:, LANES:].T
    half = (_W_IK - _W_MG) // 2
    ga_ref[...] = jax.nn.sigmoid(seg(_W_MG, _W_MG + half))
    gb_ref[...] = jax.nn.sigmoid(seg(_W_MG + half, _W_IK))
    ikn = _norm64(seg(_W_IK, _W_MISC), gik_ref[...], s64)
    ikt_ref[0] = ikn.T[:HEAD_DIM, :]
    ika_ref[...] = ikn.astype(BF16)
    misc = seg(_W_MISC, _W_TOTAL)
    iw_ref[...] = misc * (IDX_HEADS ** -0.5)
    ng_ref[...] = jax.nn.sigmoid(misc)


def _pad_heads(w, by_group):
    d = w.shape[0]
    w = w.reshape(d, MIX_HEADS, HEAD_DIM)
    z = jnp.zeros_like(w)
    if by_group:
        grp = (jnp.arange(MIX_HEADS) // GROUP_HEADS)[None, :, None]
        lo, hi = jnp.where(grp == 0, w, z), jnp.where(grp == 1, w, z)
    else:
        lo, hi = w, z
    return jnp.concatenate([lo, hi], axis=-1).reshape(d, MIX_HEADS * LANES)


def _proj_weights(w_in, dsa_q_norm, dsa_k_norm, idx_k_norm, nsa_q_norm, nsa_sel_k_norm, nsa_win_k_norm):
    d = w_in.shape[0]
    sizes = (512, 256, 512, 8, 64, 512, 256, 256, 256, 24, 2 * d)
    offs = np.concatenate([[0], np.cumsum(sizes)])
    dq, dkv, iq, iw, ik, nq, ckv, skv, wkv, ng, mg = [w_in[:, offs[j]:offs[j + 1]] for j in range(len(sizes))]
    misc = jnp.concatenate([iw, ng, jnp.zeros((d, LANES - 32), F32)], axis=1)
    ikp = jnp.concatenate([ik, jnp.zeros((d, LANES - HEAD_DIM), F32)], axis=1)
    w = jnp.concatenate([_pad_heads(dq, True), _pad_heads(iq, False), _pad_heads(nq, True),
                         dkv, ckv, skv, wkv, mg, ikp, misc], axis=1).astype(BF16)
    assert w.shape[1] == _W_TOTAL
    scale = HEAD_DIM ** -0.5 * LOG2E
    gqd = _pad_heads(jnp.tile(dsa_q_norm * scale, MIX_HEADS)[None, :], True)
    gqn = _pad_heads(jnp.tile(nsa_q_norm * scale, MIX_HEADS)[None, :], True)
    gk = jnp.concatenate([jnp.tile(g, 2) for g in (dsa_k_norm, nsa_sel_k_norm, nsa_win_k_norm)])[None, :]
    gik = jnp.concatenate([idx_k_norm, jnp.zeros((LANES - HEAD_DIM,), F32)])[None, :]
    return w, gqd, gqn, gk, gik


def _proj(h, g_mix, pw, nb, tm, q_dtype, with_vt):
    n, d = h.shape
    s = n // nb
    nt = s // tm
    w, gqd, gqn, gk, gik = pw
    row = lambda width: pl.BlockSpec((tm, width), lambda b, j: (b * nt + j, 0))
    colmaj = lambda width: pl.BlockSpec((1, width, tm), lambda b, j: (b, 0, j))
    const = lambda shape: pl.BlockSpec(shape, lambda b, j: (0,) * len(shape))
    tok = lambda width, dt: (row(width), jax.ShapeDtypeStruct((n, width), dt))
    feat = lambda width: (colmaj(width), jax.ShapeDtypeStruct((nb, width, s), F32))
    outs = [("qd", tok(1024, q_dtype)), ("iq", tok(1024, q_dtype)), ("qn", tok(1024, q_dtype))]
    for p in ("d", "s", "w"):
        outs += [(p + "kvt", feat(256)), (p + "k", tok(LANES, BF16)), (p + "v", tok(LANES, BF16))]
        if with_vt:
            outs.append((p + "vt", (pl.BlockSpec((1, tm // TQ, LANES, TQ), lambda b, j: (b, j, 0, 0)),
                                    jax.ShapeDtypeStruct((nb, s // TQ, LANES, TQ), BF16))))
    outs += [("ckv", tok(256, F32)), ("ckvt", feat(256)), ("ga", tok(d, F32)), ("gb", tok(d, F32)),
             ("ikt", feat(HEAD_DIM)), ("ika", tok(LANES, BF16)), ("iw", tok(LANES, F32)), ("ng", tok(LANES, F32))]
    res = pl.pallas_call(
        functools.partial(_proj_kernel, with_vt=with_vt),
        grid=(nb, nt),
        in_specs=[row(d), const((1, d)), const(w.shape), const(gqd.shape), const(gqn.shape),
                  const(gk.shape), const(gik.shape), const((LANES, LANES))],
        out_specs=[spec for _, (spec, _) in outs],
        out_shape=[shape for _, (_, shape) in outs],
        scratch_shapes=[pltpu.VMEM((tm, d), BF16)],
        compiler_params=_cparams("arbitrary", "arbitrary"),
        name="proj",
    )(h, g_mix.reshape(1, d), w, gqd, gqn, gk, gik, _seg_matrix())
    return {name: r for (name, _), r in zip(outs, res)}


def _bias_tile(dist, valid, tab_ref, h):
    n = jnp.maximum(dist, 0)
    big = jnp.full(n.shape, MAX_EXACT, I32)
    for thr in BUCKET_THR:
        big = big + (n >= thr).astype(I32)
    bucket = jnp.where(n < MAX_EXACT, n, big)
    far = tab_ref[NUM_BUCKETS - 1, h]
    out = jnp.zeros(n.shape, F32)
    for b in range(NUM_BUCKETS - 1):
        out = jnp.where(bucket == b, (tab_ref[b, h] - far) * LOG2E, out)
    return jnp.where(valid, out, NEG)


def _cmp_block_of(u, n_cmp):
    half = n_cmp // 2
    return 2 * jnp.where(u >= half, u - half, u) + (u >= half).astype(I32)


def _tiles_kernel(tab_ref, diag_ref, prev_ref, slast_ref, snew_ref, swin_ref, scmp_ref, *, n_cmp):
    h = pl.program_id(0)
    kr = _iota2((TQ, TQ), 0)
    qc = _iota2((TQ, TQ), 1)
    diag_ref[0] = _bias_tile(qc - kr, qc >= kr, tab_ref, h)
    prev_ref[0] = _bias_tile(TQ + qc - kr, qc >= -1, tab_ref, h)
    r8 = _iota2((DS_PAD, PAGE), 0)
    c8 = _iota2((DS_PAD, PAGE), 1)
    slast_ref[0] = _bias_tile(PAGE + r8 - c8, r8 >= -1, tab_ref, h)
    snew_ref[0] = _bias_tile(r8 - c8, r8 >= c8, tab_ref, h)
    rw = _iota2((DS_PAD, WINDOW), 0)
    cw = _iota2((DS_PAD, WINDOW), 1)
    swin_ref[0] = _bias_tile(WINDOW + rw - cw, cw > rw, tab_ref, h)
    rc = _iota2((DS_PAD, n_cmp), 0)
    jc = _cmp_block_of(_iota2((DS_PAD, n_cmp), 1), n_cmp)
    scmp_ref[0] = _bias_tile(n_cmp * CMP_BLOCK + rc - (jc * CMP_BLOCK + CMP_BLOCK - 1), rc >= -1, tab_ref, h)


def _bias_tiles(rel_bias, n_cmp_sample):
    nh = rel_bias.shape[1]
    per_head = lambda *s: pl.BlockSpec((1,) + s, lambda h: (h,) + (0,) * len(s))
    shapes = [(TQ, TQ), (TQ, TQ), (DS_PAD, PAGE), (DS_PAD, PAGE), (DS_PAD, WINDOW), (DS_PAD, n_cmp_sample)]
    return pl.pallas_call(
        functools.partial(_tiles_kernel, n_cmp=n_cmp_sample),
        grid=(nh,),
        in_specs=[pl.BlockSpec(memory_space=pltpu.SMEM)],
        out_specs=[per_head(*s) for s in shapes],
        out_shape=[jax.ShapeDtypeStruct((nh,) + s, F32) for s in shapes],
        compiler_params=_cparams("arbitrary"),
        name="bias_tiles",
    )(rel_bias)


def _cmp_tiles_kernel(tab_ref, o_ref, *, head0, n_cmp):
    h = pl.program_id(0) + head0
    i = pl.program_id(1)
    j = _cmp_block_of(_iota2((n_cmp, TQ), 0), n_cmp)
    dist = i * TQ + _iota2((n_cmp, TQ), 1) - (j * CMP_BLOCK + CMP_BLOCK - 1)
    o_ref[0, 0] = _bias_tile(dist, dist >= 0, tab_ref, h)


def _cmp_tiles(rel_bias, head0, n_qblk, n_cmp):
    return pl.pallas_call(
        functools.partial(_cmp_tiles_kernel, head0=head0, n_cmp=n_cmp),
        grid=(MIX_HEADS, n_qblk),
        in_specs=[pl.BlockSpec(memory_space=pltpu.SMEM)],
        out_specs=pl.BlockSpec((1, 1, n_cmp, TQ), lambda h, i: (i, h, 0, 0)),
        out_shape=jax.ShapeDtypeStruct((n_qblk, MIX_HEADS, n_cmp, TQ), F32),
        compiler_params=_cparams("arbitrary", "arbitrary"),
        name="cmp_tiles",
    )(rel_bias)


def _sortable(x):
    b = lax.bitcast_convert_type(x, I32)
    return jnp.where(x == 0.0, 0, b ^ ((b >> 31) & 0x7FFFFFFF))


def _kth_largest(count_ge, shape, k, n_keys):
    zero = jnp.zeros(shape, I32)
    c0 = count_ge(zero)
    t0 = jnp.where(c0 >= k, zero, jnp.full(shape, INT_MIN, I32))
    done0 = ((c0 == k) | ((c0 < k) & (n_keys == k))).astype(I32)

    def cond(st):
        return (st[0] < 31) & (st[3] == 0)

    def body(st):
        bi, t, done, _ = st
        cand = t + jnp.left_shift(jnp.int32(1), 30 - bi)
        cnt = count_ge(cand)
        t = jnp.where((done == 0) & (cnt >= k), cand, t)
        done = jnp.where(cnt == k, 1, done)
        return bi + 1, t, done, jnp.min(done)

    return lax.while_loop(cond, body, (jnp.int32(0), t0, done0, jnp.min(done0)))[1]


def _fold_rows(x):
    return x.reshape(x.shape[0] // SUBLANES, SUBLANES, x.shape[1]).sum(axis=0)


def _fold_lanes(x):
    out = x[:, :LANES]
    for j in range(1, x.shape[1] // LANES):
        out = out + x[:, j * LANES:(j + 1) * LANES]
    return out


def _rep_lanes(x, width):
    n = width // LANES
    return x if n == 1 else jnp.concatenate([x] * n, axis=1)


def _stack_heads(src, dst_ref, heads, rows):
    for h in range(heads):
        dst_ref[h * rows:(h + 1) * rows, :] = src[:, h * LANES:(h + 1) * LANES].astype(dst_ref.dtype)


def _tri(n, lower):
    r = np.arange(n)
    m = (r[None, :] < r[:, None]) if lower else (r[:, None] < r[None, :])
    return jnp.asarray(m.astype(np.float32), BF16)


def _select_bias_t(keys, thr, need, run_ref, ltri):
    eq = keys == thr
    eq_f = jnp.where(eq, 1.0, 0.0)
    before = _dot(ltri, eq_f.astype(BF16)) + run_ref[...]
    tie_ok = before < need
    run_ref[...] += jnp.sum(eq_f, axis=0, keepdims=True)
    return jnp.where(keys > thr, 0.0, jnp.where(eq, jnp.where(tie_ok, 0.0, NEG), NEG))


ACC_ROWS = LANES + 16


def _flash_init_t(m_ref, acc_ref):
    m_ref[...] = jnp.full(m_ref.shape, -3e38, F32)
    acc_ref[...] = jnp.zeros_like(acc_ref)


def _flash_update_t(lg, vt, m_ref, acc_ref):
    m_old = m_ref[...]
    m_new = jnp.maximum(m_old, jnp.max(lg, axis=0, keepdims=True))
    alpha = jnp.exp2(m_old - m_new)
    p = jnp.exp2(lg - m_new).astype(BF16)
    vta = jnp.concatenate([vt, jnp.ones((ACC_ROWS - LANES, vt.shape[1]), BF16)], axis=0)
    acc_ref[...] = alpha * acc_ref[...] + _dot(vta, p)
    m_ref[...] = m_new


def _flash_result_t(acc_ref):
    return acc_ref[:LANES, :] * (1.0 / acc_ref[LANES:LANES + 1, :])


def _head_logits(kc, qs_ref, heads, add_of):
    lg = _dot_nt(kc, qs_ref[...])
    return jnp.concatenate([lg[:, r * TQ:(r + 1) * TQ] + add_of(r) for r in range(heads)], axis=1)


def _key_rows(ref, c):
    return ref[0, pl.ds(pl.multiple_of(c * TQ, TQ), TQ), :]


def _dsa_prompt_kernel(qa_ref, iqa_ref, iw_ref, ika_ref, k_ref, vt_ref, tdiag_ref, tprev_ref, ltri_ref,
                       o_ref, keys_ref, qs_ref, iqs_ref, wt_ref, run_ref, m_ref, acc_ref, *, k_top):
    i = pl.program_id(1)
    nh = MIX_HEADS
    _stack_heads(qa_ref[0], qs_ref, nh, TQ)
    _stack_heads(iqa_ref[0], iqs_ref, IDX_HEADS, TQ)
    wt_ref[...] = iw_ref[0].T

    def score_chunk(c, causal):
        s = _dot_nt(_key_rows(ika_ref, c), iqs_ref[...])
        sc = jnp.zeros((TQ, TQ), F32)
        for h in range(IDX_HEADS):
            sc = sc + jnp.maximum(s[:, h * TQ:(h + 1) * TQ], 0.0) * wt_ref[MISC_IW + h:MISC_IW + h + 1, :]
        if causal:
            sc = jnp.where(_iota2((TQ, TQ), 0) <= _iota2((TQ, TQ), 1), sc, NEG)
        keys_ref[c] = _sortable(sc)

    def score_body(c, carry):
        score_chunk(c, False)
        return carry

    lax.fori_loop(0, i, score_body, 0)
    score_chunk(i, True)

    def count_ge(cand):
        def body(c, acc):
            return acc + _fold_rows((keys_ref[c] >= cand).astype(I32))

        acc = lax.fori_loop(0, i + 1, body, jnp.zeros((SUBLANES, TQ), I32))
        return jnp.sum(acc, axis=0, keepdims=True)

    thr = _kth_largest(count_ge, (1, TQ), k_top, (i + 1) * TQ)
    need = (k_top - count_ge(thr + 1)).astype(F32)
    has_tie = jnp.max(count_ge(thr)) > k_top

    _flash_init_t(m_ref, acc_ref)
    run_ref[...] = jnp.zeros_like(run_ref)

    def attend(c, tile_ref):
        keys = keys_ref[c]
        sel = lax.cond(has_tie,
                       lambda: _select_bias_t(keys, thr, need, run_ref, ltri_ref[...]),
                       lambda: jnp.where(keys >= thr, 0.0, NEG))
        add_of = (lambda h: sel) if tile_ref is None else (lambda h: sel + tile_ref[h])
        lg = _head_logits(_key_rows(k_ref, c), qs_ref, nh, add_of)
        _flash_update_t(lg, vt_ref[0, c], m_ref, acc_ref)

    def far_body(c, carry):
        attend(c, None)
        return carry

    lax.fori_loop(0, jnp.maximum(i - 1, 0), far_body, 0)

    @pl.when(i >= 1)
    def _():
        attend(i - 1, tprev_ref)

    attend(i, tdiag_ref)
    ot = _flash_result_t(acc_ref)
    for h in range(nh):
        o_ref[0, :, h * LANES:(h + 1) * LANES] = ot[:, h * TQ:(h + 1) * TQ].T.astype(o_ref.dtype)


def _dsa_prompt(m, tdiag, tprev, b, s):
    nq = s // TQ
    k_top = min(IDX_TOPK, s // 4)
    assert TQ >= k_top
    qblk = lambda width: pl.BlockSpec((1, TQ, width), lambda bi, i: (bi, i, 0))
    seq = lambda width: pl.BlockSpec((1, s, width), lambda bi, i: (bi, 0, 0))
    r3 = lambda a: a.reshape(b, s, a.shape[-1])
    cols = MIX_HEADS * TQ
    return pl.pallas_call(
        functools.partial(_dsa_prompt_kernel, k_top=k_top),
        grid=(b, nq),
        in_specs=[qblk(1024), qblk(1024), qblk(LANES), seq(LANES), seq(LANES),
                  pl.BlockSpec((1, nq, LANES, TQ), lambda bi, i: (bi, 0, 0, 0)),
                  _full((MIX_HEADS, TQ, TQ)), _full((MIX_HEADS, TQ, TQ)), _full((TQ, TQ))],
        out_specs=qblk(1024),
        out_shape=jax.ShapeDtypeStruct((b, s, MIX_HEADS * LANES), BF16),
        scratch_shapes=[pltpu.VMEM((nq, TQ, TQ), I32),
                        pltpu.VMEM((cols, LANES), BF16),
                        pltpu.VMEM((IDX_HEADS * TQ, LANES), BF16),
                        pltpu.VMEM((LANES, TQ), F32),
                        pltpu.VMEM((1, TQ), F32),
                        pltpu.VMEM((1, cols), F32),
                        pltpu.VMEM((ACC_ROWS, cols), F32)],
        compiler_params=_cparams("arbitrary", "arbitrary"),
        name="dsa_prompt",
    )(r3(m["qd"]), r3(m["iq"]), r3(m["iw"]), r3(m["ika"]), r3(m["dk"]), m["dvt"],
      tdiag, tprev, _tri(TQ, True))


def _compress_weights(w_phi, g_ck):
    eye = jnp.eye(N_GROUPS, dtype=F32)
    wbig = jnp.einsum("clde,cx,gy->lcgdxye", w_phi, jnp.eye(2, dtype=F32), eye)
    wbig = wbig.reshape(CMP_BLOCK * 4 * HEAD_DIM, 4 * HEAD_DIM).astype(BF16)
    return wbig, jnp.tile(g_ck, 2)[None, :]


def _compress_kernel(x_ref, wbig_ref, gck_ref, s64_ref, ck_ref, cvt_ref):
    kdim = wbig_ref.shape[0]
    half = x_ref.shape[1]
    for p in range(2):
        c = _dot(x_ref[0, :, p * kdim:(p + 1) * kdim].astype(BF16), wbig_ref[...])
        ck_ref[0, p * half:(p + 1) * half, :] = _norm64(c[:, :LANES], gck_ref[...], s64_ref[...]).astype(BF16)
        cvt_ref[0, :, p * half:(p + 1) * half] = c[:, LANES:].T.astype(BF16)


def _compress_prompt(ckv, cw, b, s):
    wbig, gck = cw
    nc = s // CMP_BLOCK
    kdim = wbig.shape[0]
    x = ckv.reshape(b, nc // 2, 2 * kdim)
    return pl.pallas_call(
        _compress_kernel,
        grid=(b,),
        in_specs=[pl.BlockSpec((1, nc // 2, 2 * kdim), lambda bi: (bi, 0, 0)), _full(wbig.shape),
                  _full((1, LANES)), _full((LANES, LANES))],
        out_specs=[pl.BlockSpec((1, nc, LANES), lambda bi: (bi, 0, 0)),
                   pl.BlockSpec((1, LANES, nc), lambda bi: (bi, 0, 0))],
        out_shape=[jax.ShapeDtypeStruct((b, nc, LANES), BF16), jax.ShapeDtypeStruct((b, LANES, nc), BF16)],
        compiler_params=_cparams("arbitrary"),
        name="compress_prompt",
    )(x, wbig, gck, _seg_matrix())


def _pick_blocks(score, n_sel, axis):
    pos = _iota2(score.shape, axis)
    picked = jnp.zeros(score.shape, F32)
    for _ in range(n_sel):
        top = jnp.max(score, axis=axis, keepdims=True)
        first = jnp.min(jnp.where(score == top, pos, score.shape[axis]), axis=axis, keepdims=True)
        hit = pos == first
        picked = jnp.where(hit, 1.0, picked)
        score = jnp.where(hit, -jnp.inf, score)
    return picked


def _softmax_valid(lg, axis):
    valid = lg > 0.5 * NEG
    p = jnp.where(valid, jnp.exp2(lg - jnp.max(lg, axis=axis, keepdims=True)), 0.0)
    tot = jnp.sum(p, axis=axis, keepdims=True)
    return p * (1.0 / jnp.where(tot > 0.0, tot, 1.0))


def _nsa_prompt_kernel(qa_ref, ng_ref, ck_ref, cvt_ref, ks_ref, vst_ref, kw_ref, vwt_ref,
                       tcmp_ref, tdiag_ref, tprev_ref, tw2_ref, exp_ref,
                       o_ref, qs_ref, ngt_ref, oc_ref, os_ref, m_ref, acc_ref, *, n_sel):
    i = pl.program_id(1)
    gh = GROUP_HEADS
    nh = MIX_HEADS
    n_cmp = ck_ref.shape[1]
    n_blk = n_cmp // 2
    ngt_ref[...] = ng_ref[0].T
    blk = _iota2((n_blk, TQ), 0)
    cur = (i * TQ + _iota2((n_blk, TQ), 1)) // SEL_BLOCK
    forced = (blk == 0) | (blk == cur) | (blk == cur - 1)

    _stack_heads(qa_ref[0], qs_ref, nh, TQ)

    pc = _softmax_valid(_head_logits(ck_ref[0], qs_ref, nh, lambda h: tcmp_ref[0, h]), 0)
    oc_ref[...] = _dot(cvt_ref[0], pc.astype(BF16))
    scores = []
    for g in range(N_GROUPS):
        imp = pc[:, g * gh * TQ:(g * gh + 1) * TQ]
        for r in range(1, gh):
            imp = imp + pc[:, (g * gh + r) * TQ:(g * gh + r + 1) * TQ]
        imp = imp[:n_blk] + imp[n_blk:]
        scores.append(jnp.where(forced, BIG, jnp.where(blk <= cur, imp, NEG)))
    picked = _pick_blocks(jnp.concatenate(scores, axis=1), n_sel, 0).astype(BF16)

    def branch_chunk(k_ref, vt_ref, c, tile_ref, extra_of):
        if tile_ref is None:
            add_of = extra_of
        elif extra_of is None:
            add_of = lambda h: tile_ref[h]
        else:
            add_of = lambda h: tile_ref[h] + extra_of(h)
        lg = _head_logits(_key_rows(k_ref, c), qs_ref, nh, add_of)
        _flash_update_t(lg, vt_ref[0, c], m_ref, acc_ref)

    _flash_init_t(m_ref, acc_ref)

    def sel_chunk(c, tile_ref):
        keep = jnp.where(_dot(exp_ref[c], picked) > 0.5, 0.0, NEG)
        branch_chunk(ks_ref, vst_ref, c, tile_ref, lambda h: keep[:, (h // gh) * TQ:(h // gh + 1) * TQ])

    def far_body(c, carry):
        sel_chunk(c, None)
        return carry

    lax.fori_loop(0, jnp.maximum(i - 1, 0), far_body, 0)

    @pl.when(i >= 1)
    def _():
        sel_chunk(i - 1, tprev_ref)

    sel_chunk(i, tdiag_ref)
    os_ref[...] = _flash_result_t(acc_ref)

    _flash_init_t(m_ref, acc_ref)

    @pl.when(i >= 2)
    def _():
        edge = tw2_ref[...]
        branch_chunk(kw_ref, vwt_ref, i - 2, None, lambda h: edge)

    @pl.when(i >= 1)
    def _():
        branch_chunk(kw_ref, vwt_ref, i - 1, tprev_ref, None)

    branch_chunk(kw_ref, vwt_ref, i, tdiag_ref, None)
    ow = _flash_result_t(acc_ref)
    for h in range(nh):
        cs = slice(h * TQ, (h + 1) * TQ)
        gate = lambda j: ngt_ref[MISC_NG + 3 * h + j:MISC_NG + 3 * h + j + 1, :]
        ot = gate(0) * oc_ref[:, cs] + gate(1) * os_ref[:, cs] + gate(2) * ow[:, cs]
        o_ref[0, :, h * LANES:(h + 1) * LANES] = ot.T.astype(o_ref.dtype)


def _block_expander_t(n_blk, n_keys):
    key_blk = np.arange(n_keys) // SEL_BLOCK
    e = (key_blk[:, None] == np.arange(n_blk)[None, :]).astype(np.float32)
    return jnp.asarray(e.reshape(n_keys // TQ, TQ, n_blk), BF16)


def _window_edge_tile_t():
    r = np.arange(TQ)
    return jnp.asarray(np.where(r[:, None] > r[None, :], 0.0, NEG).astype(np.float32))


def _nsa_prompt(m, ck, cvt, tcmp, tdiag, tprev, b, s):
    nq = s // TQ
    n_cmp = s // CMP_BLOCK
    n_blk = s // SEL_BLOCK
    n_sel = min(N_SEL, n_blk)
    assert WINDOW == 2 * TQ
    qblk = lambda width: pl.BlockSpec((1, TQ, width), lambda bi, i: (bi, i, 0))
    seq = lambda rows, width: pl.BlockSpec((1, rows, width), lambda bi, i: (bi, 0, 0))
    seqt = pl.BlockSpec((1, nq, LANES, TQ), lambda bi, i: (bi, 0, 0, 0))
    r3 = lambda a: a.reshape(b, s, a.shape[-1])
    cols = MIX_HEADS * TQ
    return pl.pallas_call(
        functools.partial(_nsa_prompt_kernel, n_sel=n_sel),
        grid=(b, nq),
        in_specs=[qblk(1024), qblk(LANES), seq(n_cmp, LANES), seq(LANES, n_cmp),
                  seq(s, LANES), seqt, seq(s, LANES), seqt,
                  pl.BlockSpec((1, MIX_HEADS, n_cmp, TQ), lambda bi, i: (i, 0, 0, 0)),
                  _full((MIX_HEADS, TQ, TQ)), _full((MIX_HEADS, TQ, TQ)), _full((TQ, TQ)),
                  _full((nq, TQ, n_blk))],
        out_specs=qblk(1024),
        out_shape=jax.ShapeDtypeStruct((b, s, MIX_HEADS * LANES), BF16),
        scratch_shapes=[pltpu.VMEM((cols, LANES), BF16),
                        pltpu.VMEM((LANES, TQ), F32),
                        pltpu.VMEM((LANES, cols), F32),
                        pltpu.VMEM((LANES, cols), F32),
                        pltpu.VMEM((1, cols), F32),
                        pltpu.VMEM((ACC_ROWS, cols), F32)],
        compiler_params=_cparams("arbitrary", "arbitrary"),
        name="nsa_prompt",
    )(r3(m["qn"]), r3(m["ng"]), ck, cvt, r3(m["sk"]), m["svt"], r3(m["wk"]), m["wvt"],
      tcmp, tdiag, tprev, _window_edge_tile_t(), _block_expander_t(n_blk, s))


def _merge_kernel(h_ref, oa_ref, ob_ref, ga_ref, gb_ref, wa_ref, wb_ref, wo_ref, o_ref):
    merged = ga_ref[...] * _dot(oa_ref[...], wa_ref[...]) + gb_ref[...] * _dot(ob_ref[...], wb_ref[...])
    o_ref[...] = h_ref[...] + _dot(merged.astype(BF16), wo_ref[...])


def _pad_head_rows(w):
    return _pad_heads(w.T, True).T.astype(BF16)


def _merge(h, oa, ob, ga, gb, wa, wb, wo, tm):
    n, d = h.shape
    row = lambda width: pl.BlockSpec((tm, width), lambda i: (i, 0))
    return pl.pallas_call(
        _merge_kernel,
        grid=(n // tm,),
        in_specs=[row(d), row(1024), row(1024), row(d), row(d),
                  _full(wa.shape), _full(wb.shape), _full(wo.shape)],
        out_specs=row(d),
        out_shape=jax.ShapeDtypeStruct((n, d), F32),
        compiler_params=_cparams("arbitrary"),
        name="merge",
    )(h, oa, ob, ga, gb, wa, wb, wo)


STEP_KEYS = PAGES_PER_STEP * PAGE


def _page_specs(shape_tail, step_of):
    nd = len(shape_tail)

    def spec(pp):
        def imap(b, s, pt):
            return (pt[b, step_of(s) * PAGES_PER_STEP + pp],) + (0,) * nd
        return pl.BlockSpec((1,) + shape_tail, imap)

    return [spec(pp) for pp in range(PAGES_PER_STEP)]


def _feature_major_pages(cache):
    n, rows = cache.shape[:2]
    nd = cache.ndim
    return cache.transpose((0,) + tuple(range(2, nd)) + (1,)).reshape(n, -1, rows)


def _cat_pages(refs, lo, hi):
    return jnp.concatenate([r[0, lo:hi, :].astype(BF16) for r in refs], axis=1)


def _pad_rows(x, pad_ref, transpose=False):
    pad_ref[...] = jnp.zeros_like(pad_ref)
    pad_ref[0:DS_PAD, :] = x
    full = pad_ref[...]
    return (full.T if transpose else full).astype(BF16)


def _select_bias_rows(keys, thr, need, run_ref, utri, ones):
    out = []
    for j in range(keys.shape[1] // PAGE):
        kj = keys[:, j * PAGE:(j + 1) * PAGE]
        eq = kj == thr
        eq_b = jnp.where(eq, 1.0, 0.0).astype(BF16)
        tie_ok = _dot(eq_b, utri) + run_ref[...] < need
        run_ref[...] += _dot(eq_b, ones)
        out.append(jnp.where(kj > thr, 0.0, jnp.where(eq, jnp.where(tie_ok, 0.0, NEG), NEG)))
    return out[0] if len(out) == 1 else jnp.concatenate(out, axis=1)


def _flash_init(m_ref, l_ref, acc_ref):
    m_ref[...] = jnp.full(m_ref.shape, -3e38, F32)
    l_ref[...] = jnp.zeros_like(l_ref)
    acc_ref[...] = jnp.zeros_like(acc_ref)


def _flash_update_rows(lg, vt, m_ref, l_ref, acc_ref):
    m_old = m_ref[...]
    m_new = jnp.maximum(m_old, jnp.max(lg, axis=-1, keepdims=True))
    alpha = jnp.exp2(m_old - m_new)
    p = jnp.exp2(lg - m_new)
    l_ref[...] = alpha * l_ref[...] + jnp.sum(p, axis=-1, keepdims=True)
    acc_ref[...] = alpha * acc_ref[...] + _dot_nt(p.astype(BF16), vt)
    m_ref[...] = m_new


def _last_page_tile(tile, on):
    t = jnp.where(on, tile, 0.0)
    if PAGES_PER_STEP == 1:
        return t
    return jnp.concatenate([jnp.zeros((DS_PAD, STEP_KEYS - PAGE), F32), t], axis=1)


def _dsa_sample_kernel(pt_ref, qa_ref, iqa_ref, iw_ref, iknew_ref, kvnew_ref, *rest, n_pages, k_top):
    npp = PAGES_PER_STEP
    idx_refs, kv_refs = rest[:npp], rest[npp:2 * npp]
    (slast_ref, snew_ref, utri_ref, ones_ref, o_ref, keys_ref, keysn_ref, qs_ref, iqs_ref, wb_ref,
     thr_ref, need_ref, run_ref, pad_ref, m_ref, l_ref, acc_ref) = rest[2 * npp:]
    s = pl.program_id(1)
    ns = n_pages // npp
    nh = MIX_HEADS
    rq = DS_PAD

    def score(sd):
        sc = jnp.zeros((rq, sd.shape[1]), F32)
        for h in range(IDX_HEADS):
            sc = sc + jnp.maximum(sd[h * rq:(h + 1) * rq], 0.0) * _rep_lanes(wb_ref[h], sd.shape[1])
        return sc

    @pl.when(s == 0)
    def _():
        _stack_heads(qa_ref[...], qs_ref, nh, rq)
        _stack_heads(iqa_ref[...], iqs_ref, IDX_HEADS, rq)
        iw = iw_ref[...]
        for h in range(IDX_HEADS):
            wb_ref[h] = jnp.broadcast_to(iw[:, MISC_IW + h:MISC_IW + h + 1], (rq, LANES))

    @pl.when(s < ns)
    def _():
        ikt = _cat_pages(idx_refs, 0, HEAD_DIM)
        keys_ref[s] = _sortable(score(_dot(iqs_ref[:, :HEAD_DIM].astype(BF16), ikt)))

    @pl.when(s == ns - 1)
    def _():
        sc = score(_dot_nt(iqs_ref[...].astype(BF16), _pad_rows(iknew_ref[...], pad_ref)))
        sc = jnp.where(_iota2((rq, PAGE), 0) >= _iota2((rq, PAGE), 1), sc, NEG)
        keysn_ref[...] = _sortable(sc)

        def count_ge(cand):
            cw = _rep_lanes(cand, STEP_KEYS)

            def body(c, acc):
                return acc + _fold_lanes((keys_ref[c] >= cw).astype(I32))

            acc = lax.fori_loop(0, ns, body, (keysn_ref[...] >= cand).astype(I32))
            return jnp.broadcast_to(jnp.sum(acc, axis=-1, keepdims=True), acc.shape)

        thr = _kth_largest(count_ge, (rq, LANES), k_top, (n_pages + 1) * PAGE)
        thr_ref[...] = thr
        need_ref[...] = (k_top - count_ge(thr + 1)).astype(F32)
        _flash_init(m_ref, l_ref, acc_ref)
        run_ref[...] = jnp.zeros_like(run_ref)

    def attend(keys, kt, vt, tile_of):
        sel = _select_bias_rows(keys, thr_ref[...], need_ref[...], run_ref, utri_ref[...], ones_ref[...])
        lg = _dot(qs_ref[...].astype(BF16), kt)
        parts = [lg[h * rq:(h + 1) * rq] + sel + tile_of(h) for h in range(nh)]
        _flash_update_rows(jnp.concatenate(parts, axis=0), vt, m_ref, l_ref, acc_ref)

    @pl.when(s >= ns)
    def _():
        is_last = s == 2 * ns - 1
        attend(keys_ref[s - ns], _cat_pages(kv_refs, 0, LANES), _cat_pages(kv_refs, LANES, 2 * LANES),
               lambda h: _last_page_tile(slast_ref[h], is_last))

    @pl.when(s == 2 * ns - 1)
    def _():
        kvn = kvnew_ref[...]
        ktn = _pad_rows(kvn[:, :LANES], pad_ref, True)
        vtn = _pad_rows(kvn[:, LANES:], pad_ref, True)
        attend(keysn_ref[...], ktn, vtn, lambda h: snew_ref[h])
        inv = 1.0 / l_ref[...]
        for h in range(nh):
            rows = slice(h * rq, (h + 1) * rq)
            o_ref[:, h * LANES:(h + 1) * LANES] = acc_ref[rows, :] * inv[rows]


def _dsa_sample(m, kvnew, cache_idx, cache_kv, page_table, slast, snew, db):
    n_pages = page_table.shape[1]
    ns = n_pages // PAGES_PER_STEP
    assert ns * PAGES_PER_STEP == n_pages
    k_top = min(IDX_TOPK, (n_pages * PAGE + 4) // 4)
    row = lambda width: pl.BlockSpec((DS_PAD, width), lambda b, s, pt: (b, 0))
    const = lambda shape: pl.BlockSpec(shape, lambda b, s, pt: (0,) * len(shape))
    rows = MIX_HEADS * DS_PAD
    grid_spec = pltpu.PrefetchScalarGridSpec(
        num_scalar_prefetch=1,
        grid=(db, 2 * ns),
        in_specs=[row(1024), row(1024), row(LANES), row(LANES), row(2 * LANES)]
        + _page_specs((HEAD_DIM, PAGE), lambda s: jnp.minimum(s, ns - 1))
        + _page_specs((2 * LANES, PAGE), lambda s: jnp.maximum(s - ns, 0))
        + [const((MIX_HEADS, DS_PAD, PAGE)), const((MIX_HEADS, DS_PAD, PAGE)), const((PAGE, PAGE)),
           const((PAGE, PAGE))],
        out_specs=row(1024),
        scratch_shapes=[pltpu.VMEM((ns, DS_PAD, STEP_KEYS), I32), pltpu.VMEM((DS_PAD, PAGE), I32),
                        pltpu.VMEM((rows, LANES), F32), pltpu.VMEM((rows, LANES), F32),
                        pltpu.VMEM((IDX_HEADS, DS_PAD, LANES), F32),
                        pltpu.VMEM((DS_PAD, LANES), I32), pltpu.VMEM((DS_PAD, LANES), F32),
                        pltpu.VMEM((DS_PAD, LANES), F32), pltpu.VMEM((PAGE, LANES), F32),
                        pltpu.VMEM((rows, 1), F32), pltpu.VMEM((rows, 1), F32),
                        pltpu.VMEM((rows, LANES), F32)])
    idx_pages = _feature_major_pages(cache_idx)
    kv_pages = _feature_major_pages(cache_kv)
    return pl.pallas_call(
        functools.partial(_dsa_sample_kernel, n_pages=n_pages, k_top=k_top),
        grid_spec=grid_spec,
        out_shape=jax.ShapeDtypeStruct((db * DS_PAD, 1024), F32),
        compiler_params=_cparams("arbitrary", "arbitrary"),
        name="dsa_sample",
    )(page_table, m["qd"], m["iq"], m["iw"], m["ika"].astype(F32), kvnew,
      *([idx_pages] * PAGES_PER_STEP), *([kv_pages] * PAGES_PER_STEP), slast, snew,
      _tri(PAGE, False), jnp.ones((PAGE, PAGE), BF16))


def _compress_sample_kernel(pt_ref, *rest, n_pages):
    npp = PAGES_PER_STEP
    page_refs = rest[:npp]
    wbig_ref, gck_ref, s64_ref, ck_ref, cv_ref, x_ref, nat_ref = rest[npp:]
    s = pl.program_id(1)
    for pp in range(npp):
        r0 = pl.multiple_of((s * npp + pp) * PAGE, PAGE)
        for j in range(2):
            x_ref[j, pl.ds(r0, PAGE), :] = page_refs[pp][0, j * LANES:(j + 1) * LANES, :].T

    @pl.when(s == n_pages // npp - 1)
    def _():
        n_cmp = n_pages * PAGE // CMP_BLOCK
        width = 2 * LANES
        acc = jnp.zeros((n_cmp, width), F32)
        for l in range(CMP_BLOCK):
            rows = pl.ds(l, n_cmp, stride=CMP_BLOCK)
            xl = jnp.concatenate([x_ref[0, rows, :], x_ref[1, rows, :]], axis=1).astype(BF16)
            acc = acc + _dot(xl, wbig_ref[l * width:(l + 1) * width, :])
        nat_ref[0] = acc[:, :LANES]
        nat_ref[1] = acc[:, LANES:]
        half = n_cmp // 2
        for p in range(2):
            rows = pl.ds(p, half, stride=2)
            ck_ref[0, p * half:(p + 1) * half, :] = _norm64(nat_ref[0, rows, :], gck_ref[...],
                                                            s64_ref[...]).astype(BF16)
            cv_ref[0, p * half:(p + 1) * half, :] = nat_ref[1, rows, :].astype(BF16)


def _compress_sample(cache_cmp, page_table, cw, db):
    wbig, gck = cw
    n_pages = page_table.shape[1]
    n_cmp = n_pages * PAGE // CMP_BLOCK
    const = lambda shape: pl.BlockSpec(shape, lambda b, s, pt: (0,) * len(shape))
    out = jax.ShapeDtypeStruct((db, n_cmp, LANES), BF16)
    grid_spec = pltpu.PrefetchScalarGridSpec(
        num_scalar_prefetch=1,
        grid=(db, n_pages // PAGES_PER_STEP),
        in_specs=_page_specs((2 * LANES, PAGE), lambda s: s)
        + [const(wbig.shape), const((1, LANES)), const((LANES, LANES))],
        out_specs=[pl.BlockSpec((1, n_cmp, LANES), lambda b, s, pt: (b, 0, 0))] * 2,
        scratch_shapes=[pltpu.VMEM((2, n_pages * PAGE, LANES), F32), pltpu.VMEM((2, n_cmp, LANES), F32)])
    return pl.pallas_call(
        functools.partial(_compress_sample_kernel, n_pages=n_pages),
        grid_spec=grid_spec,
        out_shape=[out, out],
        compiler_params=_cparams("arbitrary", "arbitrary"),
        name="compress_sample",
    )(page_table, *([_feature_major_pages(cache_cmp)] * PAGES_PER_STEP), wbig, gck, _seg_matrix())


def _nsa_sample_kernel(pt_ref, qa_ref, ng_ref, ck_ref, cv_ref, win_ref, sknew_ref, wknew_ref, *rest,
                       n_pages, n_sel):
    npp = PAGES_PER_STEP
    sel_refs = rest[:npp]
    (scmp_ref, slast_ref, snew_ref, swin_ref, exp_ref, o_ref,
     qs_ref, picked_ref, oc_ref, pad_ref, m_ref, l_ref, acc_ref) = rest[npp:]
    s = pl.program_id(1)
    ns = n_pages // npp
    nh = MIX_HEADS
    gh = GROUP_HEADS
    rq = DS_PAD
    n_cmp = ck_ref.shape[1]
    n_past_blk = n_cmp // 2
    nb = picked_ref.shape[2]

    def tiled(lg, tile_of, extra=None):
        parts = []
        for h in range(nh):
            part = lg[h * rq:(h + 1) * rq] + tile_of(h)
            if extra is not None:
                part = part + extra[h // gh]
            parts.append(part)
        return jnp.concatenate(parts, axis=0)

    @pl.when(s == 0)
    def _():
        _stack_heads(qa_ref[...], qs_ref, nh, rq)
        lc = tiled(_dot_nt(qs_ref[...].astype(BF16), ck_ref[0]), lambda h: scmp_ref[h])
        pc = _softmax_valid(lc, 1)
        oc_ref[...] = _dot(pc.astype(BF16), cv_ref[0])
        blk = _iota2((rq, nb), 1)
        forced = (blk == 0) | (blk == n_past_blk) | (blk == n_past_blk - 1)
        for g in range(N_GROUPS):
            imp = pc[g * gh * rq:(g * gh + 1) * rq]
            for r in range(1, gh):
                imp = imp + pc[(g * gh + r) * rq:(g * gh + r + 1) * rq]
            imp = imp[:, :n_past_blk] + imp[:, n_past_blk:]
            imp = jnp.concatenate([imp, jnp.zeros((rq, nb - n_past_blk), F32)], axis=1)
            score = jnp.where(forced, BIG, jnp.where(blk < n_past_blk, imp, -jnp.inf))
            picked_ref[g] = _pick_blocks(score, n_sel, 1)
        _flash_init(m_ref, l_ref, acc_ref)

    keep = [jnp.where(_dot(picked_ref[g].astype(BF16), exp_ref[s]) > 0.5, 0.0, NEG) for g in range(N_GROUPS)]
    lg = _dot(qs_ref[...].astype(BF16), _cat_pages(sel_refs, 0, LANES))
    is_last = s == ns - 1
    _flash_update_rows(tiled(lg, lambda h: _last_page_tile(slast_ref[h], is_last), keep),
                       _cat_pages(sel_refs, LANES, 2 * LANES), m_ref, l_ref, acc_ref)

    @pl.when(s == ns - 1)
    def _():
        ng = ng_ref[...]
        new_tile = lambda h: snew_ref[h]
        skn = sknew_ref[...]
        ktn = _pad_rows(skn[:, :LANES], pad_ref, True)
        vtn = _pad_rows(skn[:, LANES:], pad_ref, True)
        _flash_update_rows(tiled(_dot(qs_ref[...].astype(BF16), ktn), new_tile), vtn, m_ref, l_ref, acc_ref)
        o_s = acc_ref[...] * (1.0 / l_ref[...])
        _flash_init(m_ref, l_ref, acc_ref)
        lg = _dot(qs_ref[...].astype(BF16), win_ref[0, :LANES, :].astype(BF16))
        _flash_update_rows(tiled(lg, lambda h: swin_ref[h]), win_ref[0, LANES:, :].astype(BF16),
                           m_ref, l_ref, acc_ref)
        wkn = wknew_ref[...]
        ktn = _pad_rows(wkn[:, :LANES], pad_ref, True)
        vtn = _pad_rows(wkn[:, LANES:], pad_ref, True)
        _flash_update_rows(tiled(_dot(qs_ref[...].astype(BF16), ktn), new_tile), vtn, m_ref, l_ref, acc_ref)
        o_w = acc_ref[...] * (1.0 / l_ref[...])
        o_c = oc_ref[...]
        for h in range(nh):
            sl = slice(h * rq, (h + 1) * rq)
            gate = lambda j: ng[:, MISC_NG + 3 * h + j:MISC_NG + 3 * h + j + 1]
            o_ref[:, h * LANES:(h + 1) * LANES] = gate(0) * o_c[sl] + gate(1) * o_s[sl] + gate(2) * o_w[sl]


def _sample_block_expander(n_pages, nb):
    key_blk = np.arange(n_pages * PAGE) // SEL_BLOCK
    e = (np.arange(nb)[:, None] == key_blk[None, :]).astype(np.float32)
    return jnp.asarray(e.reshape(nb, n_pages // PAGES_PER_STEP, STEP_KEYS).transpose(1, 0, 2), BF16)


def _nsa_sample(m, sknew, wknew, ck, cv, cache_sel, win_t, page_table, scmp, slast, snew, swin, db):
    n_pages = page_table.shape[1]
    ns = n_pages // PAGES_PER_STEP
    n_cmp = ck.shape[1]
    n_sblk = n_cmp // 2 + 1
    n_sel = min(N_SEL, n_sblk)
    nb = -(-n_sblk // LANES) * LANES
    assert win_t.shape[2] == WINDOW
    row = lambda width: pl.BlockSpec((DS_PAD, width), lambda b, s, pt: (b, 0))
    seq = lambda r, width: pl.BlockSpec((1, r, width), lambda b, s, pt: (b, 0, 0))
    const = lambda shape: pl.BlockSpec(shape, lambda b, s, pt: (0,) * len(shape))
    rows = MIX_HEADS * DS_PAD
    grid_spec = pltpu.PrefetchScalarGridSpec(
        num_scalar_prefetch=1,
        grid=(db, ns),
        in_specs=[row(1024), row(LANES), seq(n_cmp, LANES), seq(n_cmp, LANES), seq(2 * LANES, WINDOW),
                  row(2 * LANES), row(2 * LANES)]
        + _page_specs((2 * LANES, PAGE), lambda s: s)
        + [const((MIX_HEADS, DS_PAD, n_cmp)), const((MIX_HEADS, DS_PAD, PAGE)),
           const((MIX_HEADS, DS_PAD, PAGE)), const((MIX_HEADS, DS_PAD, WINDOW)), const((ns, nb, STEP_KEYS))],
        out_specs=row(1024),
        scratch_shapes=[pltpu.VMEM((rows, LANES), F32), pltpu.VMEM((N_GROUPS, DS_PAD, nb), F32),
                        pltpu.VMEM((rows, LANES), F32), pltpu.VMEM((PAGE, LANES), F32),
                        pltpu.VMEM((rows, 1), F32), pltpu.VMEM((rows, 1), F32),
                        pltpu.VMEM((rows, LANES), F32)])
    return pl.pallas_call(
        functools.partial(_nsa_sample_kernel, n_pages=n_pages, n_sel=n_sel),
        grid_spec=grid_spec,
        out_shape=jax.ShapeDtypeStruct((db * DS_PAD, 1024), F32),
        compiler_params=_cparams("arbitrary", "arbitrary"),
        name="nsa_sample",
    )(page_table, m["qn"], m["ng"], ck, cv, win_t, sknew, wknew,
      *([_feature_major_pages(cache_sel)] * PAGES_PER_STEP), scmp, slast, snew, swin,
      _sample_block_expander(n_pages, nb))


def _kv_leaf(t):
    nb, _, s = t.shape
    return t.reshape(nb, 2, N_GROUPS, HEAD_DIM, s).transpose(0, 4, 1, 2, 3)


def kernel(x_prompt, x_sample, cache_dsa_kv, cache_dsa_idx_k, cache_nsa_cmp_kv, cache_nsa_sel_kv,
           state_nsa_win_kv, page_table, rel_bias, ffn1_norm, ffn1_w_gu, ffn1_w_down, mix_norm, w_in,
           dsa_q_norm, dsa_k_norm, idx_k_norm, nsa_q_norm, nsa_cmp_k_norm, nsa_sel_k_norm, nsa_win_k_norm,
           nsa_w_phi, w_dsa_o, w_nsa_o, w_out, ffn2_norm, ffn2_w_gu, ffn2_w_down):
    b, s, d = x_prompt.shape
    db, ds, _ = x_sample.shape
    depth = ffn1_norm.shape[0]
    n_pages = page_table.shape[1]
    past = n_pages * PAGE
    assert s % TQ == 0 and ds <= DS_PAD and ds < CMP_BLOCK and past >= WINDOW
    n_p, n_s = b * s, db * DS_PAD
    tm_p, tm_s = min(512, s), min(512, n_s)
    tm_ffn = min(1024, n_p)

    hp = x_prompt.reshape(n_p, d)
    hs = jnp.pad(x_sample, ((0, 0), (0, DS_PAD - ds), (0, 0))).reshape(n_s, d)
    dsa_h = slice(0, MIX_HEADS)
    nsa_h = slice(MIX_HEADS, 2 * MIX_HEADS)
    tdiag, tprev, slast, snew, swin, scmp = _bias_tiles(rel_bias, past // CMP_BLOCK)
    tcmp = _cmp_tiles(rel_bias, MIX_HEADS, s // TQ, s // CMP_BLOCK)

    outs_p = [[] for _ in range(5)]
    outs_s = [[] for _ in range(5)]
    for l in range(depth):
        pw = _proj_weights(w_in[l], dsa_q_norm[l], dsa_k_norm[l], idx_k_norm[l], nsa_q_norm[l],
                           nsa_sel_k_norm[l], nsa_win_k_norm[l])
        cw = _compress_weights(nsa_w_phi[l], nsa_cmp_k_norm[l])
        wa, wb = _pad_head_rows(w_dsa_o[l]), _pad_head_rows(w_nsa_o[l])
        wo = w_out[l].astype(BF16)

        hp = _ffn(hp, ffn1_norm[l], ffn1_w_gu[l], ffn1_w_down[l], tm_ffn)
        mp = _proj(hp, mix_norm[l], pw, b, tm_p, BF16, True)
        oa = _dsa_prompt(mp, tdiag[dsa_h], tprev[dsa_h], b, s)
        ck, cvt = _compress_prompt(mp["ckv"], cw, b, s)
        ob = _nsa_prompt(mp, ck, cvt, tcmp, tdiag[nsa_h], tprev[nsa_h], b, s)
        hp = _merge(hp, oa.reshape(n_p, -1), ob.reshape(n_p, -1), mp["ga"], mp["gb"], wa, wb, wo, tm_p)
        hp = _ffn(hp, ffn2_norm[l], ffn2_w_gu[l], ffn2_w_down[l], tm_ffn)
        win_len = min(WINDOW, s)
        for lst, val in zip(outs_p, (_kv_leaf(mp["dkvt"]), mp["ikt"].transpose(0, 2, 1), _kv_leaf(mp["ckvt"]),
                                     _kv_leaf(mp["skvt"]), _kv_leaf(mp["wkvt"][:, :, s - win_len:]))):
            lst.append(val)

        hs = _ffn(hs, ffn1_norm[l], ffn1_w_gu[l], ffn1_w_down[l], tm_s)
        ms = _proj(hs, mix_norm[l], pw, 1, tm_s, F32, False)
        new_rows = lambda name: ms[name][0].T
        new_t = lambda name: ms[name][0].reshape(-1, db, DS_PAD)[:, :, :ds].transpose(1, 0, 2)
        oa_s = _dsa_sample(ms, new_rows("dkvt"), cache_dsa_idx_k[l], cache_dsa_kv[l], page_table,
                           slast[dsa_h], snew[dsa_h], db)
        ck_s, cv_s = _compress_sample(cache_nsa_cmp_kv[l], page_table, cw, db)
        win_t = _feature_major_pages(state_nsa_win_kv[l])
        ob_s = _nsa_sample(ms, new_rows("skvt"), new_rows("wkvt"), ck_s, cv_s, cache_nsa_sel_kv[l], win_t,
                           page_table, scmp[nsa_h], slast[nsa_h], snew[nsa_h], swin[nsa_h], db)
        hs = _merge(hs, oa_s.astype(BF16), ob_s.astype(BF16), ms["ga"], ms["gb"], wa, wb, wo, tm_s)
        hs = _ffn(hs, ffn2_norm[l], ffn2_w_gu[l], ffn2_w_down[l], tm_s)
        win_new = jnp.concatenate([win_t, new_t("wkvt")], axis=2)
        win_new = win_new[:, :, win_new.shape[2] - min(WINDOW, past + ds):]
        for lst, val in zip(outs_s, (_kv_leaf(new_t("dkvt")), new_t("ikt").transpose(0, 2, 1),
                                     _kv_leaf(new_t("ckvt")), _kv_leaf(new_t("skvt")), _kv_leaf(win_new))):
            lst.append(val)

    y_p = hp.reshape(b, s, d)
    y_s = hs.reshape(db, DS_PAD, d)[:, :ds]
    return (y_p, y_s) + tuple(jnp.stack(o) for o in outs_p) + tuple(jnp.stack(o) for o in outs_s)
```

```python
import functools
import math

import numpy as np
import jax
import jax.numpy as jnp
from jax import lax
from jax.experimental import pallas as pl
from jax.experimental.pallas import tpu as pltpu

F32 = jnp.float32
BF16 = jnp.bfloat16
I32 = jnp.int32

HEAD_DIM = 64
LANES = 128
SUBLANES = 8
N_GROUPS = 2
GROUP_HEADS = 4
MIX_HEADS = N_GROUPS * GROUP_HEADS
IDX_HEADS = 8
IDX_TOPK = 256
CMP_BLOCK = 32
SEL_BLOCK = 64
N_SEL = 16
WINDOW = 512
NUM_BUCKETS = 32
MAX_EXACT = 16
MAX_DISTANCE = 128
PAGE = 128
EPS = 1e-6
NEG = -1e30
BIG = 1e30
TQ = 256
DS_PAD = 8
PAGES_PER_STEP = 16
VMEM_LIMIT = 56 * 1024 * 1024

INT_MIN = -(2 ** 31)
I16 = jnp.int16
I16_MIN, I16_MAX = -(2 ** 15), 2 ** 15 - 1
PACKED_ROWS = 2 * SUBLANES
LOG2E = math.log2(math.e)


def _bucket_thresholds():
    n = np.arange(MAX_EXACT, 4 * MAX_DISTANCE)
    ratio = np.log(n.astype(np.float32) / np.float32(MAX_EXACT)) / np.float32(math.log(MAX_DISTANCE / MAX_EXACT))
    big = np.minimum(MAX_EXACT + (ratio * np.float32(NUM_BUCKETS - MAX_EXACT)).astype(np.int32), NUM_BUCKETS - 1)
    return [int(n[np.argmax(big >= b)]) for b in range(MAX_EXACT + 1, NUM_BUCKETS)]


BUCKET_THR = _bucket_thresholds()


def _dot(a, b):
    return jnp.dot(a, b, preferred_element_type=F32)


def _dot_nt(a, b):
    return lax.dot_general(a, b, (((1,), (1,)), ((), ())), preferred_element_type=F32)


def _cparams(*sem):
    return pltpu.CompilerParams(dimension_semantics=sem, vmem_limit_bytes=VMEM_LIMIT)


def _full(shape):
    nd = len(shape)
    return pl.BlockSpec(shape, lambda *_: (0,) * nd)


def _resident(shape):
    nd = len(shape)
    return pl.BlockSpec(shape, lambda *_: (0,) * nd, pipeline_mode=pl.Buffered(1))


def _iota2(shape, axis):
    return lax.broadcasted_iota(I32, shape, axis)


def _rms_rows(x, g):
    return x * lax.rsqrt(jnp.mean(x * x, axis=-1, keepdims=True) + EPS) * g


def _norm64(z, gain, s64):
    outs = []
    for j in range(z.shape[1] // LANES):
        zb = z[:, j * LANES:(j + 1) * LANES]
        sq = zb * zb
        hi = sq.astype(BF16)
        lo = (sq - hi.astype(F32)).astype(BF16)
        ms = (_dot(hi, s64) + _dot(lo, s64)) * (1.0 / HEAD_DIM)
        outs.append(zb * lax.rsqrt(ms + EPS) * gain[:, j * LANES:(j + 1) * LANES])
    return outs[0] if len(outs) == 1 else jnp.concatenate(outs, axis=1)


def _seg_matrix():
    r = np.arange(LANES) // HEAD_DIM
    return jnp.asarray((r[:, None] == r[None, :]).astype(np.float32), BF16)


def _ffn_kernel(x_ref, g_ref, wg_ref, wu_ref, wd_ref, o_ref, xn_ref, acc_ref):
    x = x_ref[...]
    xn_ref[...] = _rms_rows(x, g_ref[...]).astype(BF16)
    acc_ref[...] = jnp.zeros_like(acc_ref)

    def body(c, carry):
        xn = xn_ref[...]
        a = _dot(xn, wg_ref[c])
        b = _dot(xn, wu_ref[c])
        act = (a * jax.nn.sigmoid(a) * b).astype(BF16)
        acc_ref[...] += _dot(act, wd_ref[c])
        return carry

    lax.fori_loop(0, wg_ref.shape[0], body, 0)
    o_ref[...] = x + 0.5 * acc_ref[...]


def _ffn(x, g, w_gu, w_dn, tm):
    n, d = x.shape
    d_ff = w_dn.shape[0]
    fc = 256
    nc = d_ff // fc
    assert nc * fc == d_ff and n % tm == 0
    wg = w_gu[:, :d_ff].reshape(d, nc, fc).transpose(1, 0, 2).astype(BF16)
    wu = w_gu[:, d_ff:].reshape(d, nc, fc).transpose(1, 0, 2).astype(BF16)
    wd = w_dn.reshape(nc, fc, d).astype(BF16)
    return pl.pallas_call(
        _ffn_kernel,
        grid=(n // tm,),
        in_specs=[pl.BlockSpec((tm, d), lambda i: (i, 0)), _full((1, d)),
                  _resident((nc, d, fc)), _resident((nc, d, fc)), _resident((nc, fc, d))],
        out_specs=pl.BlockSpec((tm, d), lambda i: (i, 0)),
        out_shape=jax.ShapeDtypeStruct((n, d), F32),
        scratch_shapes=[pltpu.VMEM((tm, d), BF16), pltpu.VMEM((tm, d), F32)],
        compiler_params=_cparams("arbitrary"),
        name="ffn",
    )(x, g.reshape(1, d), wg, wu, wd)


_W_QD, _W_IQ, _W_QN = 0, 1024, 2048
_W_DKV, _W_CKV, _W_SKV, _W_WKV = 3072, 3328, 3584, 3840
_W_MG, _W_IK, _W_MISC, _W_TOTAL = 4096, 6144, 6272, 6400
MISC_IW = 0
MISC_NG = 8


def _proj_kernel(h_ref, gm_ref, w_ref, gqd_ref, gqn_ref, gk_ref, gik_ref, s64_ref, *out_refs, with_vt):
    refs = list(out_refs)
    hn_ref = refs.pop()
    qd_ref, iq_ref, qn_ref = refs[:3]
    kv_refs = refs[3:3 + (4 if with_vt else 3) * 3]
    ckv_ref, ckvt_ref, ga_ref, gb_ref, ikt_ref, ika_ref, iw_ref, ng_ref = refs[len(refs) - 8:]
    hn_ref[...] = _rms_rows(h_ref[...], gm_ref[...]).astype(BF16)
    s64 = s64_ref[...]
    tm = h_ref.shape[0]

    def seg(a, b):
        return _dot(hn_ref[...], w_ref[:, a:b])

    qd_ref[...] = _norm64(seg(_W_QD, _W_IQ), gqd_ref[...], s64).astype(qd_ref.dtype)
    iq_ref[...] = (seg(_W_IQ, _W_QN) * (HEAD_DIM ** -0.5)).astype(iq_ref.dtype)
    qn_ref[...] = _norm64(seg(_W_QN, _W_DKV), gqn_ref[...], s64).astype(qn_ref.dtype)
    per = 4 if with_vt else 3
    for j, a in enumerate((_W_DKV, _W_SKV, _W_WKV)):
        kvt_ref, k_ref, v_ref = kv_refs[j * per:j * per + 3]
        z = seg(a, a + 2 * LANES)
        kn = _norm64(z[:, :LANES], gk_ref[:, j * LANES:(j + 1) * LANES], s64)
        vt = z[:, LANES:].T
        kvt_ref[0, :LANES, :] = kn.T
        kvt_ref[0, LANES:, :] = vt
        k_ref[...] = kn.astype(BF16)
        v_ref[...] = z[:, LANES:].astype(BF16)
        if with_vt:
            vt_ref = kv_refs[j * per + 3]
            for cc in range(tm // TQ):
                vt_ref[0, cc] = vt[:, cc * TQ:(cc + 1) * TQ].astype(BF16)
    zc = seg(_W_CKV, _W_SKV)
    ckv_ref[...] = zc
    ckvt_ref[0, :LANES, :] = zc[:, :LANES].T
    ckvt_ref[0, LANES:, :] = zc[:, LANES:].T
    half = (_W_IK - _W_MG) // 2
    ga_ref[...] = jax.nn.sigmoid(seg(_W_MG, _W_MG + half))
    gb_ref[...] = jax.nn.sigmoid(seg(_W_MG + half, _W_IK))
    ikn = _norm64(seg(_W_IK, _W_MISC), gik_ref[...], s64)
    ikt_ref[0] = ikn.T[:HEAD_DIM, :]
    ika_ref[...] = ikn.astype(BF16)
    misc = seg(_W_MISC, _W_TOTAL)
    iw_ref[...] = misc * (IDX_HEADS ** -0.5)
    ng_ref[...] = jax.nn.sigmoid(misc)


def _pad_heads(w, by_group):
    d = w.shape[0]
    w = w.reshape(d, MIX_HEADS, HEAD_DIM)
    z = jnp.zeros_like(w)
    if by_group:
        grp = (jnp.arange(MIX_HEADS) // GROUP_HEADS)[None, :, None]
        lo, hi = jnp.where(grp == 0, w, z), jnp.where(grp == 1, w, z)
    else:
        lo, hi = w, z
    return jnp.concatenate([lo, hi], axis=-1).reshape(d, MIX_HEADS * LANES)


def _proj_weights(w_in, dsa_q_norm, dsa_k_norm, idx_k_norm, nsa_q_norm, nsa_sel_k_norm, nsa_win_k_norm):
    d = w_in.shape[0]
    sizes = (512, 256, 512, 8, 64, 512, 256, 256, 256, 24, 2 * d)
    offs = np.concatenate([[0], np.cumsum(sizes)])
    dq, dkv, iq, iw, ik, nq, ckv, skv, wkv, ng, mg = [w_in[:, offs[j]:offs[j + 1]] for j in range(len(sizes))]
    misc = jnp.concatenate([iw, ng, jnp.zeros((d, LANES - 32), F32)], axis=1)
    ikp = jnp.concatenate([ik, jnp.zeros((d, LANES - HEAD_DIM), F32)], axis=1)
    w = jnp.concatenate([_pad_heads(dq, True), _pad_heads(iq, False), _pad_heads(nq, True),
                         dkv, ckv, skv, wkv, mg, ikp, misc], axis=1).astype(BF16)
    assert w.shape[1] == _W_TOTAL
    scale = HEAD_DIM ** -0.5 * LOG2E
    gqd = _pad_heads(jnp.tile(dsa_q_norm * scale, MIX_HEADS)[None, :], True)
    gqn = _pad_heads(jnp.tile(nsa_q_norm * scale, MIX_HEADS)[None, :], True)
    gk = jnp.concatenate([jnp.tile(g, 2) for g in (dsa_k_norm, nsa_sel_k_norm, nsa_win_k_norm)])[None, :]
    gik = jnp.concatenate([idx_k_norm, jnp.zeros((LANES - HEAD_DIM,), F32)])[None, :]
    return w, gqd, gqn, gk, gik


def _proj(h, g_mix, pw, nb, tm, q_dtype, with_vt):
    n, d = h.shape
    s = n // nb
    nt = s // tm
    w, gqd, gqn, gk, gik = pw
    row = lambda width: pl.BlockSpec((tm, width), lambda b, j: (b * nt + j, 0))
    colmaj = lambda width: pl.BlockSpec((1, width, tm), lambda b, j: (b, 0, j))
    const = lambda shape: pl.BlockSpec(shape, lambda b, j: (0,) * len(shape))
    tok = lambda width, dt: (row(width), jax.ShapeDtypeStruct((n, width), dt))
    feat = lambda width: (colmaj(width), jax.ShapeDtypeStruct((nb, width, s), F32))
    outs = [("qd", tok(1024, q_dtype)), ("iq", tok(1024, q_dtype)), ("qn", tok(1024, q_dtype))]
    for p in ("d", "s", "w"):
        outs += [(p + "kvt", feat(256)), (p + "k", tok(LANES, BF16)), (p + "v", tok(LANES, BF16))]
        if with_vt:
            outs.append((p + "vt", (pl.BlockSpec((1, tm // TQ, LANES, TQ), lambda b, j: (b, j, 0, 0)),
                                    jax.ShapeDtypeStruct((nb, s // TQ, LANES, TQ), BF16))))
    outs += [("ckv", tok(256, F32)), ("ckvt", feat(256)), ("ga", tok(d, F32)), ("gb", tok(d, F32)),
             ("ikt", feat(HEAD_DIM)), ("ika", tok(LANES, BF16)), ("iw", tok(LANES, F32)), ("ng", tok(LANES, F32))]
    res = pl.pallas_call(
        functools.partial(_proj_kernel, with_vt=with_vt),
        grid=(nb, nt),
        in_specs=[row(d), const((1, d)), const(w.shape), const(gqd.shape), const(gqn.shape),
                  const(gk.shape), const(gik.shape), const((LANES, LANES))],
        out_specs=[spec for _, (spec, _) in outs],
        out_shape=[shape for _, (_, shape) in outs],
        scratch_shapes=[pltpu.VMEM((tm, d), BF16)],
        compiler_params=_cparams("arbitrary", "arbitrary"),
        name="proj",
    )(h, g_mix.reshape(1, d), w, gqd, gqn, gk, gik, _seg_matrix())
    return {name: r for (name, _), r in zip(outs, res)}


def _bias_tile(dist, valid, tab_ref, h):
    n = jnp.maximum(dist, 0)
    big = jnp.full(n.shape, MAX_EXACT, I32)
    for thr in BUCKET_THR:
        big = big + (n >= thr).astype(I32)
    bucket = jnp.where(n < MAX_EXACT, n, big)
    far = tab_ref[NUM_BUCKETS - 1, h]
    out = jnp.zeros(n.shape, F32)
    for b in range(NUM_BUCKETS - 1):
        out = jnp.where(bucket == b, (tab_ref[b, h] - far) * LOG2E, out)
    return jnp.where(valid, out, NEG)


def _cmp_block_of(u, n_cmp):
    half = n_cmp // 2
    return 2 * jnp.where(u >= half, u - half, u) + (u >= half).astype(I32)


def _tiles_kernel(tab_ref, diag_ref, prev_ref, slast_ref, snew_ref, swin_ref, scmp_ref, *, n_cmp):
    h = pl.program_id(0)
    kr = _iota2((TQ, TQ), 0)
    qc = _iota2((TQ, TQ), 1)
    diag_ref[0] = _bias_tile(qc - kr, qc >= kr, tab_ref, h)
    prev_ref[0] = _bias_tile(TQ + qc - kr, qc >= -1, tab_ref, h)
    r8 = _iota2((DS_PAD, PAGE), 0)
    c8 = _iota2((DS_PAD, PAGE), 1)
    slast_ref[0] = _bias_tile(PAGE + r8 - c8, r8 >= -1, tab_ref, h)
    snew_ref[0] = _bias_tile(r8 - c8, r8 >= c8, tab_ref, h)
    rw = _iota2((DS_PAD, WINDOW), 0)
    cw = _iota2((DS_PAD, WINDOW), 1)
    swin_ref[0] = _bias_tile(WINDOW + rw - cw, cw > rw, tab_ref, h)
    rc = _iota2((DS_PAD, n_cmp), 0)
    jc = _cmp_block_of(_iota2((DS_PAD, n_cmp), 1), n_cmp)
    scmp_ref[0] = _bias_tile(n_cmp * CMP_BLOCK + rc - (jc * CMP_BLOCK + CMP_BLOCK - 1), rc >= -1, tab_ref, h)


def _bias_tiles(rel_bias, n_cmp_sample):
    nh = rel_bias.shape[1]
    per_head = lambda *s: pl.BlockSpec((1,) + s, lambda h: (h,) + (0,) * len(s))
    shapes = [(TQ, TQ), (TQ, TQ), (DS_PAD, PAGE), (DS_PAD, PAGE), (DS_PAD, WINDOW), (DS_PAD, n_cmp_sample)]
    return pl.pallas_call(
        functools.partial(_tiles_kernel, n_cmp=n_cmp_sample),
        grid=(nh,),
        in_specs=[pl.BlockSpec(memory_space=pltpu.SMEM)],
        out_specs=[per_head(*s) for s in shapes],
        out_shape=[jax.ShapeDtypeStruct((nh,) + s, F32) for s in shapes],
        compiler_params=_cparams("arbitrary"),
        name="bias_tiles",
    )(rel_bias)


def _cmp_tiles_kernel(tab_ref, o_ref, *, head0, n_cmp):
    h = pl.program_id(0) + head0
    i = pl.program_id(1)
    j = _cmp_block_of(_iota2((n_cmp, TQ), 0), n_cmp)
    dist = i * TQ + _iota2((n_cmp, TQ), 1) - (j * CMP_BLOCK + CMP_BLOCK - 1)
    o_ref[0, 0] = _bias_tile(dist, dist >= 0, tab_ref, h)


def _cmp_tiles(rel_bias, head0, n_qblk, n_cmp):
    return pl.pallas_call(
        functools.partial(_cmp_tiles_kernel, head0=head0, n_cmp=n_cmp),
        grid=(MIX_HEADS, n_qblk),
        in_specs=[pl.BlockSpec(memory_space=pltpu.SMEM)],
        out_specs=pl.BlockSpec((1, 1, n_cmp, TQ), lambda h, i: (i, h, 0, 0)),
        out_shape=jax.ShapeDtypeStruct((n_qblk, MIX_HEADS, n_cmp, TQ), F32),
        compiler_params=_cparams("arbitrary", "arbitrary"),
        name="cmp_tiles",
    )(rel_bias)


def _sortable(x):
    b = lax.bitcast_convert_type(x, I32)
    return jnp.where(x == 0.0, 0, b ^ ((b >> 31) & 0x7FFFFFFF))


def _bit_search(count_ge, shape, k, lowest, bits):
    zero = jnp.zeros(shape, I32)
    t0 = jnp.where(count_ge(zero) >= k, zero, jnp.full(shape, lowest, I32))

    def bit_body(bi, t):
        cand = t + jnp.left_shift(jnp.int32(1), bits - 2 - bi)
        return jnp.where(count_ge(cand) >= k, cand, t)

    return lax.fori_loop(0, bits - 1, bit_body, t0)


def _kth_largest(count_ge, shape, k):
    return _bit_search(count_ge, shape, k, INT_MIN, 32)


def _kth_largest16(count_ge16, k):
    return _bit_search(count_ge16, (1, TQ), k, I16_MIN, 16)


def _fold_rows(x):
    return x.reshape(x.shape[0] // SUBLANES, SUBLANES, x.shape[1]).sum(axis=0)


def _fold_lanes(x):
    out = x[:, :LANES]
    for j in range(1, x.shape[1] // LANES):
        out = out + x[:, j * LANES:(j + 1) * LANES]
    return out


def _rep_lanes(x, width):
    n = width // LANES
    return x if n == 1 else jnp.concatenate([x] * n, axis=1)


def _stack_heads(src, dst_ref, heads, rows):
    for h in range(heads):
        dst_ref[h * rows:(h + 1) * rows, :] = src[:, h * LANES:(h + 1) * LANES].astype(dst_ref.dtype)


def _tri(n, lower):
    r = np.arange(n)
    m = (r[None, :] < r[:, None]) if lower else (r[:, None] < r[None, :])
    return jnp.asarray(m.astype(np.float32), BF16)


def _select_bias_t(keys, thr, need, run_ref, ltri):
    eq = keys == thr
    eq_f = jnp.where(eq, 1.0, 0.0)
    before = _dot(ltri, eq_f.astype(BF16)) + run_ref[...]
    tie_ok = before < need
    run_ref[...] += jnp.sum(eq_f, axis=0, keepdims=True)
    return jnp.where(keys > thr, 0.0, jnp.where(eq, jnp.where(tie_ok, 0.0, NEG), NEG))


ACC_ROWS = LANES + 16


def _flash_init_t(m_ref, acc_ref):
    m_ref[...] = jnp.full(m_ref.shape, -3e38, F32)
    acc_ref[...] = jnp.zeros_like(acc_ref)


def _flash_update_t(lg, vt, m_ref, acc_ref):
    m_old = m_ref[...]
    m_new = jnp.maximum(m_old, jnp.max(lg, axis=0, keepdims=True))
    alpha = jnp.exp2(m_old - m_new)
    p = jnp.exp2(lg - m_new).astype(BF16)
    vta = jnp.concatenate([vt, jnp.ones((ACC_ROWS - LANES, vt.shape[1]), BF16)], axis=0)
    acc_ref[...] = alpha * acc_ref[...] + _dot(vta, p)
    m_ref[...] = m_new


def _flash_result_t(acc_ref):
    return acc_ref[:LANES, :] * (1.0 / acc_ref[LANES:LANES + 1, :])


def _head_logits(kc, qs_ref, heads, add_of):
    lg = _dot_nt(kc, qs_ref[...])
    return jnp.concatenate([lg[:, r * TQ:(r + 1) * TQ] + add_of(r) for r in range(heads)], axis=1)


def _key_rows(ref, c):
    return ref[0, pl.ds(pl.multiple_of(c * TQ, TQ), TQ), :]


def _dsa_prompt_kernel(qa_ref, iqa_ref, iw_ref, ika_ref, k_ref, vt_ref, tdiag_ref, tprev_ref, ltri_ref,
                       o_ref, keys_ref, half_ref, qs_ref, iqs_ref, wt_ref, run_ref, m_ref, acc_ref, *, k_top):
    i = pl.program_id(1)
    nh = MIX_HEADS
    _stack_heads(qa_ref[0], qs_ref, nh, TQ)
    _stack_heads(iqa_ref[0], iqs_ref, IDX_HEADS, TQ)
    wt_ref[...] = iw_ref[0].T

    def score_chunk(c, causal):
        s = _dot_nt(_key_rows(ika_ref, c), iqs_ref[...])
        sc = jnp.zeros((TQ, TQ), F32)
        for h in range(IDX_HEADS):
            sc = sc + jnp.maximum(s[:, h * TQ:(h + 1) * TQ], 0.0) * wt_ref[MISC_IW + h:MISC_IW + h + 1, :]
        if causal:
            sc = jnp.where(_iota2((TQ, TQ), 0) <= _iota2((TQ, TQ), 1), sc, NEG)
        key = _sortable(sc)
        keys_ref[c] = key
        half_ref[c] = (key >> 16).astype(I16)

    def score_body(c, carry):
        score_chunk(c, False)
        return carry

    lax.fori_loop(0, i, score_body, 0)
    score_chunk(i, True)

    def count_ge(cand):
        def body(c, acc):
            return acc + _fold_rows((keys_ref[c] >= cand).astype(I32))

        acc = lax.fori_loop(0, i + 1, body, jnp.zeros((SUBLANES, TQ), I32))
        return jnp.sum(acc, axis=0, keepdims=True)

    def count_ge16(cand):
        c16 = cand.astype(I16)

        def body(c, acc):
            hit = jnp.where(half_ref[c] >= c16, jnp.int16(1), jnp.int16(0))
            for j in range(TQ // PACKED_ROWS):
                acc = acc + hit[j * PACKED_ROWS:(j + 1) * PACKED_ROWS]
            return acc

        acc = lax.fori_loop(0, i + 1, body, jnp.zeros((PACKED_ROWS, TQ), I16))
        return jnp.sum(acc.astype(I32), axis=0, keepdims=True)

    hi = _kth_largest16(count_ge16, k_top)
    above = jnp.where(hi < I16_MAX, count_ge16(jnp.minimum(hi + 1, I16_MAX)), 0)

    def low_body(c, carry):
        key = keys_ref[c]
        low = (key & 0xFFFF) + I16_MIN
        half_ref[c] = jnp.where((key >> 16) == hi, low, I16_MIN).astype(I16)
        return carry

    lax.fori_loop(0, i + 1, low_body, 0)
    thr = hi * 65536 + (_kth_largest16(count_ge16, k_top - above) - I16_MIN)
    need = (k_top - count_ge(thr + 1)).astype(F32)
    has_tie = jnp.max(count_ge(thr)) > k_top

    _flash_init_t(m_ref, acc_ref)
    run_ref[...] = jnp.zeros_like(run_ref)

    def attend(c, tile_ref):
        keys = keys_ref[c]
        sel = lax.cond(has_tie,
                       lambda: _select_bias_t(keys, thr, need, run_ref, ltri_ref[...]),
                       lambda: jnp.where(keys >= thr, 0.0, NEG))
        add_of = (lambda h: sel) if tile_ref is None else (lambda h: sel + tile_ref[h])
        lg = _head_logits(_key_rows(k_ref, c), qs_ref, nh, add_of)
        _flash_update_t(lg, vt_ref[0, c], m_ref, acc_ref)

    def far_body(c, carry):
        attend(c, None)
        return carry

    lax.fori_loop(0, jnp.maximum(i - 1, 0), far_body, 0)

    @pl.when(i >= 1)
    def _():
        attend(i - 1, tprev_ref)

    attend(i, tdiag_ref)
    ot = _flash_result_t(acc_ref)
    for h in range(nh):
        o_ref[0, :, h * LANES:(h + 1) * LANES] = ot[:, h * TQ:(h + 1) * TQ].T.astype(o_ref.dtype)


def _dsa_prompt(m, tdiag, tprev, b, s):
    nq = s // TQ
    k_top = min(IDX_TOPK, s // 4)
    assert TQ >= k_top
    qblk = lambda width: pl.BlockSpec((1, TQ, width), lambda bi, i: (bi, i, 0))
    seq = lambda width: pl.BlockSpec((1, s, width), lambda bi, i: (bi, 0, 0))
    r3 = lambda a: a.reshape(b, s, a.shape[-1])
    cols = MIX_HEADS * TQ
    return pl.pallas_call(
        functools.partial(_dsa_prompt_kernel, k_top=k_top),
        grid=(b, nq),
        in_specs=[qblk(1024), qblk(1024), qblk(LANES), seq(LANES), seq(LANES),
                  pl.BlockSpec((1, nq, LANES, TQ), lambda bi, i: (bi, 0, 0, 0)),
                  _full((MIX_HEADS, TQ, TQ)), _full((MIX_HEADS, TQ, TQ)), _full((TQ, TQ))],
        out_specs=qblk(1024),
        out_shape=jax.ShapeDtypeStruct((b, s, MIX_HEADS * LANES), BF16),
        scratch_shapes=[pltpu.VMEM((nq, TQ, TQ), I32),
                        pltpu.VMEM((nq, TQ, TQ), I16),
                        pltpu.VMEM((cols, LANES), BF16),
                        pltpu.VMEM((IDX_HEADS * TQ, LANES), BF16),
                        pltpu.VMEM((LANES, TQ), F32),
                        pltpu.VMEM((1, TQ), F32),
                        pltpu.VMEM((1, cols), F32),
                        pltpu.VMEM((ACC_ROWS, cols), F32)],
        compiler_params=_cparams("arbitrary", "arbitrary"),
        name="dsa_prompt",
    )(r3(m["qd"]), r3(m["iq"]), r3(m["iw"]), r3(m["ika"]), r3(m["dk"]), m["dvt"],
      tdiag, tprev, _tri(TQ, True))


def _compress_weights(w_phi, g_ck):
    eye = jnp.eye(N_GROUPS, dtype=F32)
    wbig = jnp.einsum("clde,cx,gy->lcgdxye", w_phi, jnp.eye(2, dtype=F32), eye)
    wbig = wbig.reshape(CMP_BLOCK * 4 * HEAD_DIM, 4 * HEAD_DIM).astype(BF16)
    return wbig, jnp.tile(g_ck, 2)[None, :]


def _compress_kernel(x_ref, wbig_ref, gck_ref, s64_ref, ck_ref, cvt_ref):
    kdim = wbig_ref.shape[0]
    half = x_ref.shape[1]
    for p in range(2):
        c = _dot(x_ref[0, :, p * kdim:(p + 1) * kdim].astype(BF16), wbig_ref[...])
        ck_ref[0, p * half:(p + 1) * half, :] = _norm64(c[:, :LANES], gck_ref[...], s64_ref[...]).astype(BF16)
        cvt_ref[0, :, p * half:(p + 1) * half] = c[:, LANES:].T.astype(BF16)


def _compress_prompt(ckv, cw, b, s):
    wbig, gck = cw
    nc = s // CMP_BLOCK
    kdim = wbig.shape[0]
    x = ckv.reshape(b, nc // 2, 2 * kdim)
    return pl.pallas_call(
        _compress_kernel,
        grid=(b,),
        in_specs=[pl.BlockSpec((1, nc // 2, 2 * kdim), lambda bi: (bi, 0, 0)), _full(wbig.shape),
                  _full((1, LANES)), _full((LANES, LANES))],
        out_specs=[pl.BlockSpec((1, nc, LANES), lambda bi: (bi, 0, 0)),
                   pl.BlockSpec((1, LANES, nc), lambda bi: (bi, 0, 0))],
        out_shape=[jax.ShapeDtypeStruct((b, nc, LANES), BF16), jax.ShapeDtypeStruct((b, LANES, nc), BF16)],
        compiler_params=_cparams("arbitrary"),
        name="compress_prompt",
    )(x, wbig, gck, _seg_matrix())


def _pick_blocks(score, n_sel, axis):
    pos = _iota2(score.shape, axis)
    picked = jnp.zeros(score.shape, F32)
    for _ in range(n_sel):
        top = jnp.max(score, axis=axis, keepdims=True)
        first = jnp.min(jnp.where(score == top, pos, score.shape[axis]), axis=axis, keepdims=True)
        hit = pos == first
        picked = jnp.where(hit, 1.0, picked)
        score = jnp.where(hit, -jnp.inf, score)
    return picked


def _softmax_valid(lg, axis):
    valid = lg > 0.5 * NEG
    p = jnp.where(valid, jnp.exp2(lg - jnp.max(lg, axis=axis, keepdims=True)), 0.0)
    tot = jnp.sum(p, axis=axis, keepdims=True)
    return p * (1.0 / jnp.where(tot > 0.0, tot, 1.0))


def _nsa_prompt_kernel(qa_ref, ng_ref, ck_ref, cvt_ref, ks_ref, vst_ref, kw_ref, vwt_ref,
                       tcmp_ref, tdiag_ref, tprev_ref, tw2_ref, exp_ref,
                       o_ref, qs_ref, ngt_ref, oc_ref, os_ref, m_ref, acc_ref, *, n_sel):
    i = pl.program_id(1)
    gh = GROUP_HEADS
    nh = MIX_HEADS
    n_cmp = ck_ref.shape[1]
    n_blk = n_cmp // 2
    ngt_ref[...] = ng_ref[0].T
    blk = _iota2((n_blk, TQ), 0)
    cur = (i * TQ + _iota2((n_blk, TQ), 1)) // SEL_BLOCK
    forced = (blk == 0) | (blk == cur) | (blk == cur - 1)

    _stack_heads(qa_ref[0], qs_ref, nh, TQ)

    pc = _softmax_valid(_head_logits(ck_ref[0], qs_ref, nh, lambda h: tcmp_ref[0, h]), 0)
    oc_ref[...] = _dot(cvt_ref[0], pc.astype(BF16))
    scores = []
    for g in range(N_GROUPS):
        imp = pc[:, g * gh * TQ:(g * gh + 1) * TQ]
        for r in range(1, gh):
            imp = imp + pc[:, (g * gh + r) * TQ:(g * gh + r + 1) * TQ]
        imp = imp[:n_blk] + imp[n_blk:]
        scores.append(jnp.where(forced, BIG, jnp.where(blk <= cur, imp, NEG)))
    picked = _pick_blocks(jnp.concatenate(scores, axis=1), n_sel, 0).astype(BF16)

    def branch_chunk(k_ref, vt_ref, c, tile_ref, extra_of):
        if tile_ref is None:
            add_of = extra_of
        elif extra_of is None:
            add_of = lambda h: tile_ref[h]
        else:
            add_of = lambda h: tile_ref[h] + extra_of(h)
        lg = _head_logits(_key_rows(k_ref, c), qs_ref, nh, add_of)
        _flash_update_t(lg, vt_ref[0, c], m_ref, acc_ref)

    _flash_init_t(m_ref, acc_ref)

    def sel_chunk(c, tile_ref):
        keep = jnp.where(_dot(exp_ref[c], picked) > 0.5, 0.0, NEG)
        branch_chunk(ks_ref, vst_ref, c, tile_ref, lambda h: keep[:, (h // gh) * TQ:(h // gh + 1) * TQ])

    def far_body(c, carry):
        sel_chunk(c, None)
        return carry

    lax.fori_loop(0, jnp.maximum(i - 1, 0), far_body, 0)

    @pl.when(i >= 1)
    def _():
        sel_chunk(i - 1, tprev_ref)

    sel_chunk(i, tdiag_ref)
    os_ref[...] = _flash_result_t(acc_ref)

    _flash_init_t(m_ref, acc_ref)

    @pl.when(i >= 2)
    def _():
        edge = tw2_ref[...]
        branch_chunk(kw_ref, vwt_ref, i - 2, None, lambda h: edge)

    @pl.when(i >= 1)
    def _():
        branch_chunk(kw_ref, vwt_ref, i - 1, tprev_ref, None)

    branch_chunk(kw_ref, vwt_ref, i, tdiag_ref, None)
    ow = _flash_result_t(acc_ref)
    for h in range(nh):
        cs = slice(h * TQ, (h + 1) * TQ)
        gate = lambda j: ngt_ref[MISC_NG + 3 * h + j:MISC_NG + 3 * h + j + 1, :]
        ot = gate(0) * oc_ref[:, cs] + gate(1) * os_ref[:, cs] + gate(2) * ow[:, cs]
        o_ref[0, :, h * LANES:(h + 1) * LANES] = ot.T.astype(o_ref.dtype)


def _block_expander_t(n_blk, n_keys):
    key_blk = np.arange(n_keys) // SEL_BLOCK
    e = (key_blk[:, None] == np.arange(n_blk)[None, :]).astype(np.float32)
    return jnp.asarray(e.reshape(n_keys // TQ, TQ, n_blk), BF16)


def _window_edge_tile_t():
    r = np.arange(TQ)
    return jnp.asarray(np.where(r[:, None] > r[None, :], 0.0, NEG).astype(np.float32))


def _nsa_prompt(m, ck, cvt, tcmp, tdiag, tprev, b, s):
    nq = s // TQ
    n_cmp = s // CMP_BLOCK
    n_blk = s // SEL_BLOCK
    n_sel = min(N_SEL, n_blk)
    assert WINDOW == 2 * TQ
    qblk = lambda width: pl.BlockSpec((1, TQ, width), lambda bi, i: (bi, i, 0))
    seq = lambda rows, width: pl.BlockSpec((1, rows, width), lambda bi, i: (bi, 0, 0))
    seqt = pl.BlockSpec((1, nq, LANES, TQ), lambda bi, i: (bi, 0, 0, 0))
    r3 = lambda a: a.reshape(b, s, a.shape[-1])
    cols = MIX_HEADS * TQ
    return pl.pallas_call(
        functools.partial(_nsa_prompt_kernel, n_sel=n_sel),
        grid=(b, nq),
        in_specs=[qblk(1024), qblk(LANES), seq(n_cmp, LANES), seq(LANES, n_cmp),
                  seq(s, LANES), seqt, seq(s, LANES), seqt,
                  pl.BlockSpec((1, MIX_HEADS, n_cmp, TQ), lambda bi, i: (i, 0, 0, 0)),
                  _full((MIX_HEADS, TQ, TQ)), _full((MIX_HEADS, TQ, TQ)), _full((TQ, TQ)),
                  _full((nq, TQ, n_blk))],
        out_specs=qblk(1024),
        out_shape=jax.ShapeDtypeStruct((b, s, MIX_HEADS * LANES), BF16),
        scratch_shapes=[pltpu.VMEM((cols, LANES), BF16),
                        pltpu.VMEM((LANES, TQ), F32),
                        pltpu.VMEM((LANES, cols), F32),
                        pltpu.VMEM((LANES, cols), F32),
                        pltpu.VMEM((1, cols), F32),
                        pltpu.VMEM((ACC_ROWS, cols), F32)],
        compiler_params=_cparams("arbitrary", "arbitrary"),
        name="nsa_prompt",
    )(r3(m["qn"]), r3(m["ng"]), ck, cvt, r3(m["sk"]), m["svt"], r3(m["wk"]), m["wvt"],
      tcmp, tdiag, tprev, _window_edge_tile_t(), _block_expander_t(n_blk, s))


def _merge_kernel(h_ref, oa_ref, ob_ref, ga_ref, gb_ref, wa_ref, wb_ref, wo_ref, o_ref):
    merged = ga_ref[...] * _dot(oa_ref[...], wa_ref[...]) + gb_ref[...] * _dot(ob_ref[...], wb_ref[...])
    o_ref[...] = h_ref[...] + _dot(merged.astype(BF16), wo_ref[...])


def _pad_head_rows(w):
    return _pad_heads(w.T, True).T.astype(BF16)


def _merge(h, oa, ob, ga, gb, wa, wb, wo, tm):
    n, d = h.shape
    row = lambda width: pl.BlockSpec((tm, width), lambda i: (i, 0))
    return pl.pallas_call(
        _merge_kernel,
        grid=(n // tm,),
        in_specs=[row(d), row(1024), row(1024), row(d), row(d),
                  _full(wa.shape), _full(wb.shape), _full(wo.shape)],
        out_specs=row(d),
        out_shape=jax.ShapeDtypeStruct((n, d), F32),
        compiler_params=_cparams("arbitrary"),
        name="merge",
    )(h, oa, ob, ga, gb, wa, wb, wo)


STEP_KEYS = PAGES_PER_STEP * PAGE


def _page_specs(shape_tail, step_of):
    nd = len(shape_tail)

    def spec(pp):
        def imap(b, s, pt):
            return (pt[b, step_of(s) * PAGES_PER_STEP + pp],) + (0,) * nd
        return pl.BlockSpec((1,) + shape_tail, imap)

    return [spec(pp) for pp in range(PAGES_PER_STEP)]


def _feature_major_pages(cache):
    n, rows = cache.shape[:2]
    nd = cache.ndim
    return cache.transpose((0,) + tuple(range(2, nd)) + (1,)).reshape(n, -1, rows)


def _cat_pages(refs, lo, hi):
    return jnp.concatenate([r[0, lo:hi, :].astype(BF16) for r in refs], axis=1)


def _pad_rows(x, pad_ref, transpose=False):
    pad_ref[...] = jnp.zeros_like(pad_ref)
    pad_ref[0:DS_PAD, :] = x
    full = pad_ref[...]
    return (full.T if transpose else full).astype(BF16)


def _select_bias_rows(keys, thr, need, run_ref, utri, ones):
    out = []
    for j in range(keys.shape[1] // PAGE):
        kj = keys[:, j * PAGE:(j + 1) * PAGE]
        eq = kj == thr
        eq_b = jnp.where(eq, 1.0, 0.0).astype(BF16)
        tie_ok = _dot(eq_b, utri) + run_ref[...] < need
        run_ref[...] += _dot(eq_b, ones)
        out.append(jnp.where(kj > thr, 0.0, jnp.where(eq, jnp.where(tie_ok, 0.0, NEG), NEG)))
    return out[0] if len(out) == 1 else jnp.concatenate(out, axis=1)


def _flash_init(m_ref, l_ref, acc_ref):
    m_ref[...] = jnp.full(m_ref.shape, -3e38, F32)
    l_ref[...] = jnp.zeros_like(l_ref)
    acc_ref[...] = jnp.zeros_like(acc_ref)


def _flash_update_rows(lg, vt, m_ref, l_ref, acc_ref):
    m_old = m_ref[...]
    m_new = jnp.maximum(m_old, jnp.max(lg, axis=-1, keepdims=True))
    alpha = jnp.exp2(m_old - m_new)
    p = jnp.exp2(lg - m_new)
    l_ref[...] = alpha * l_ref[...] + jnp.sum(p, axis=-1, keepdims=True)
    acc_ref[...] = alpha * acc_ref[...] + _dot_nt(p.astype(BF16), vt)
    m_ref[...] = m_new


def _last_page_tile(tile, on):
    t = jnp.where(on, tile, 0.0)
    if PAGES_PER_STEP == 1:
        return t
    return jnp.concatenate([jnp.zeros((DS_PAD, STEP_KEYS - PAGE), F32), t], axis=1)


def _dsa_sample_kernel(pt_ref, qa_ref, iqa_ref, iw_ref, iknew_ref, kvnew_ref, *rest, n_pages, k_top):
    npp = PAGES_PER_STEP
    idx_refs, kv_refs = rest[:npp], rest[npp:2 * npp]
    (slast_ref, snew_ref, utri_ref, ones_ref, o_ref, keys_ref, keysn_ref, qs_ref, iqs_ref, wb_ref,
     thr_ref, need_ref, run_ref, pad_ref, m_ref, l_ref, acc_ref) = rest[2 * npp:]
    s = pl.program_id(1)
    ns = n_pages // npp
    nh = MIX_HEADS
    rq = DS_PAD

    def score(sd):
        sc = jnp.zeros((rq, sd.shape[1]), F32)
        for h in range(IDX_HEADS):
            sc = sc + jnp.maximum(sd[h * rq:(h + 1) * rq], 0.0) * _rep_lanes(wb_ref[h], sd.shape[1])
        return sc

    @pl.when(s == 0)
    def _():
        _stack_heads(qa_ref[...], qs_ref, nh, rq)
        _stack_heads(iqa_ref[...], iqs_ref, IDX_HEADS, rq)
        iw = iw_ref[...]
        for h in range(IDX_HEADS):
            wb_ref[h] = jnp.broadcast_to(iw[:, MISC_IW + h:MISC_IW + h + 1], (rq, LANES))

    @pl.when(s < ns)
    def _():
        ikt = _cat_pages(idx_refs, 0, HEAD_DIM)
        keys_ref[s] = _sortable(score(_dot(iqs_ref[:, :HEAD_DIM].astype(BF16), ikt)))

    @pl.when(s == ns - 1)
    def _():
        sc = score(_dot_nt(iqs_ref[...].astype(BF16), _pad_rows(iknew_ref[...], pad_ref)))
        sc = jnp.where(_iota2((rq, PAGE), 0) >= _iota2((rq, PAGE), 1), sc, NEG)
        keysn_ref[...] = _sortable(sc)

        def count_ge(cand):
            cw = _rep_lanes(cand, STEP_KEYS)

            def body(c, acc):
                return acc + _fold_lanes((keys_ref[c] >= cw).astype(I32))

            acc = lax.fori_loop(0, ns, body, (keysn_ref[...] >= cand).astype(I32))
            return jnp.broadcast_to(jnp.sum(acc, axis=-1, keepdims=True), acc.shape)

        thr = _kth_largest(count_ge, (rq, LANES), k_top)
        thr_ref[...] = thr
        need_ref[...] = (k_top - count_ge(thr + 1)).astype(F32)
        _flash_init(m_ref, l_ref, acc_ref)
        run_ref[...] = jnp.zeros_like(run_ref)

    def attend(keys, kt, vt, tile_of):
        sel = _select_bias_rows(keys, thr_ref[...], need_ref[...], run_ref, utri_ref[...], ones_ref[...])
        lg = _dot(qs_ref[...].astype(BF16), kt)
        parts = [lg[h * rq:(h + 1) * rq] + sel + tile_of(h) for h in range(nh)]
        _flash_update_rows(jnp.concatenate(parts, axis=0), vt, m_ref, l_ref, acc_ref)

    @pl.when(s >= ns)
    def _():
        is_last = s == 2 * ns - 1
        attend(keys_ref[s - ns], _cat_pages(kv_refs, 0, LANES), _cat_pages(kv_refs, LANES, 2 * LANES),
               lambda h: _last_page_tile(slast_ref[h], is_last))

    @pl.when(s == 2 * ns - 1)
    def _():
        kvn = kvnew_ref[...]
        ktn = _pad_rows(kvn[:, :LANES], pad_ref, True)
        vtn = _pad_rows(kvn[:, LANES:], pad_ref, True)
        attend(keysn_ref[...], ktn, vtn, lambda h: snew_ref[h])
        inv = 1.0 / l_ref[...]
        for h in range(nh):
            rows = slice(h * rq, (h + 1) * rq)
            o_ref[:, h * LANES:(h + 1) * LANES] = acc_ref[rows, :] * inv[rows]


def _dsa_sample(m, kvnew, cache_idx, cache_kv, page_table, slast, snew, db):
    n_pages = page_table.shape[1]
    ns = n_pages // PAGES_PER_STEP
    assert ns * PAGES_PER_STEP == n_pages
    k_top = min(IDX_TOPK, (n_pages * PAGE + 4) // 4)
    row = lambda width: pl.BlockSpec((DS_PAD, width), lambda b, s, pt: (b, 0))
    const = lambda shape: pl.BlockSpec(shape, lambda b, s, pt: (0,) * len(shape))
    rows = MIX_HEADS * DS_PAD
    grid_spec = pltpu.PrefetchScalarGridSpec(
        num_scalar_prefetch=1,
        grid=(db, 2 * ns),
        in_specs=[row(1024), row(1024), row(LANES), row(LANES), row(2 * LANES)]
        + _page_specs((HEAD_DIM, PAGE), lambda s: jnp.minimum(s, ns - 1))
        + _page_specs((2 * LANES, PAGE), lambda s: jnp.maximum(s - ns, 0))
        + [const((MIX_HEADS, DS_PAD, PAGE)), const((MIX_HEADS, DS_PAD, PAGE)), const((PAGE, PAGE)),
           const((PAGE, PAGE))],
        out_specs=row(1024),
        scratch_shapes=[pltpu.VMEM((ns, DS_PAD, STEP_KEYS), I32), pltpu.VMEM((DS_PAD, PAGE), I32),
                        pltpu.VMEM((rows, LANES), F32), pltpu.VMEM((rows, LANES), F32),
                        pltpu.VMEM((IDX_HEADS, DS_PAD, LANES), F32),
                        pltpu.VMEM((DS_PAD, LANES), I32), pltpu.VMEM((DS_PAD, LANES), F32),
                        pltpu.VMEM((DS_PAD, LANES), F32), pltpu.VMEM((PAGE, LANES), F32),
                        pltpu.VMEM((rows, 1), F32), pltpu.VMEM((rows, 1), F32),
                        pltpu.VMEM((rows, LANES), F32)])
    idx_pages = _feature_major_pages(cache_idx)
    kv_pages = _feature_major_pages(cache_kv)
    return pl.pallas_call(
        functools.partial(_dsa_sample_kernel, n_pages=n_pages, k_top=k_top),
        grid_spec=grid_spec,
        out_shape=jax.ShapeDtypeStruct((db * DS_PAD, 1024), F32),
        compiler_params=_cparams("arbitrary", "arbitrary"),
        name="dsa_sample",
    )(page_table, m["qd"], m["iq"], m["iw"], m["ika"].astype(F32), kvnew,
      *([idx_pages] * PAGES_PER_STEP), *([kv_pages] * PAGES_PER_STEP), slast, snew,
      _tri(PAGE, False), jnp.ones((PAGE, PAGE), BF16))


def _compress_sample_kernel(pt_ref, *rest, n_pages):
    npp = PAGES_PER_STEP
    page_refs = rest[:npp]
    wbig_ref, gck_ref, s64_ref, ck_ref, cv_ref, x_ref, nat_ref = rest[npp:]
    s = pl.program_id(1)
    for pp in range(npp):
        r0 = pl.multiple_of((s * npp + pp) * PAGE, PAGE)
        for j in range(2):
            x_ref[j, pl.ds(r0, PAGE), :] = page_refs[pp][0, j * LANES:(j + 1) * LANES, :].T

    @pl.when(s == n_pages // npp - 1)
    def _():
        n_cmp = n_pages * PAGE // CMP_BLOCK
        width = 2 * LANES
        acc = jnp.zeros((n_cmp, width), F32)
        for l in range(CMP_BLOCK):
            rows = pl.ds(l, n_cmp, stride=CMP_BLOCK)
            xl = jnp.concatenate([x_ref[0, rows, :], x_ref[1, rows, :]], axis=1).astype(BF16)
            acc = acc + _dot(xl, wbig_ref[l * width:(l + 1) * width, :])
        nat_ref[0] = acc[:, :LANES]
        nat_ref[1] = acc[:, LANES:]
        half = n_cmp // 2
        for p in range(2):
            rows = pl.ds(p, half, stride=2)
            ck_ref[0, p * half:(p + 1) * half, :] = _norm64(nat_ref[0, rows, :], gck_ref[...],
                                                            s64_ref[...]).astype(BF16)
            cv_ref[0, p * half:(p + 1) * half, :] = nat_ref[1, rows, :].astype(BF16)


def _compress_sample(cache_cmp, page_table, cw, db):
    wbig, gck = cw
    n_pages = page_table.shape[1]
    n_cmp = n_pages * PAGE // CMP_BLOCK
    const = lambda shape: pl.BlockSpec(shape, lambda b, s, pt: (0,) * len(shape))
    out = jax.ShapeDtypeStruct((db, n_cmp, LANES), BF16)
    grid_spec = pltpu.PrefetchScalarGridSpec(
        num_scalar_prefetch=1,
        grid=(db, n_pages // PAGES_PER_STEP),
        in_specs=_page_specs((2 * LANES, PAGE), lambda s: s)
        + [const(wbig.shape), const((1, LANES)), const((LANES, LANES))],
        out_specs=[pl.BlockSpec((1, n_cmp, LANES), lambda b, s, pt: (b, 0, 0))] * 2,
        scratch_shapes=[pltpu.VMEM((2, n_pages * PAGE, LANES), F32), pltpu.VMEM((2, n_cmp, LANES), F32)])
    return pl.pallas_call(
        functools.partial(_compress_sample_kernel, n_pages=n_pages),
        grid_spec=grid_spec,
        out_shape=[out, out],
        compiler_params=_cparams("arbitrary", "arbitrary"),
        name="compress_sample",
    )(page_table, *([_feature_major_pages(cache_cmp)] * PAGES_PER_STEP), wbig, gck, _seg_matrix())


def _nsa_sample_kernel(pt_ref, qa_ref, ng_ref, ck_ref, cv_ref, win_ref, sknew_ref, wknew_ref, *rest,
                       n_pages, n_sel):
    npp = PAGES_PER_STEP
    sel_refs = rest[:npp]
    (scmp_ref, slast_ref, snew_ref, swin_ref, exp_ref, o_ref,
     qs_ref, picked_ref, oc_ref, pad_ref, m_ref, l_ref, acc_ref) = rest[npp:]
    s = pl.program_id(1)
    ns = n_pages // npp
    nh = MIX_HEADS
    gh = GROUP_HEADS
    rq = DS_PAD
    n_cmp = ck_ref.shape[1]
    n_past_blk = n_cmp // 2
    nb = picked_ref.shape[2]

    def tiled(lg, tile_of, extra=None):
        parts = []
        for h in range(nh):
            part = lg[h * rq:(h + 1) * rq] + tile_of(h)
            if extra is not None:
                part = part + extra[h // gh]
            parts.append(part)
        return jnp.concatenate(parts, axis=0)

    @pl.when(s == 0)
    def _():
        _stack_heads(qa_ref[...], qs_ref, nh, rq)
        lc = tiled(_dot_nt(qs_ref[...].astype(BF16), ck_ref[0]), lambda h: scmp_ref[h])
        pc = _softmax_valid(lc, 1)
        oc_ref[...] = _dot(pc.astype(BF16), cv_ref[0])
        blk = _iota2((rq, nb), 1)
        forced = (blk == 0) | (blk == n_past_blk) | (blk == n_past_blk - 1)
        for g in range(N_GROUPS):
            imp = pc[g * gh * rq:(g * gh + 1) * rq]
            for r in range(1, gh):
                imp = imp + pc[(g * gh + r) * rq:(g * gh + r + 1) * rq]
            imp = imp[:, :n_past_blk] + imp[:, n_past_blk:]
            imp = jnp.concatenate([imp, jnp.zeros((rq, nb - n_past_blk), F32)], axis=1)
            score = jnp.where(forced, BIG, jnp.where(blk < n_past_blk, imp, -jnp.inf))
            picked_ref[g] = _pick_blocks(score, n_sel, 1)
        _flash_init(m_ref, l_ref, acc_ref)

    keep = [jnp.where(_dot(picked_ref[g].astype(BF16), exp_ref[s]) > 0.5, 0.0, NEG) for g in range(N_GROUPS)]
    lg = _dot(qs_ref[...].astype(BF16), _cat_pages(sel_refs, 0, LANES))
    is_last = s == ns - 1
    _flash_update_rows(tiled(lg, lambda h: _last_page_tile(slast_ref[h], is_last), keep),
                       _cat_pages(sel_refs, LANES, 2 * LANES), m_ref, l_ref, acc_ref)

    @pl.when(s == ns - 1)
    def _():
        ng = ng_ref[...]
        new_tile = lambda h: snew_ref[h]
        skn = sknew_ref[...]
        ktn = _pad_rows(skn[:, :LANES], pad_ref, True)
        vtn = _pad_rows(skn[:, LANES:], pad_ref, True)
        _flash_update_rows(tiled(_dot(qs_ref[...].astype(BF16), ktn), new_tile), vtn, m_ref, l_ref, acc_ref)
        o_s = acc_ref[...] * (1.0 / l_ref[...])
        _flash_init(m_ref, l_ref, acc_ref)
        lg = _dot(qs_ref[...].astype(BF16), win_ref[0, :LANES, :].astype(BF16))
        _flash_update_rows(tiled(lg, lambda h: swin_ref[h]), win_ref[0, LANES:, :].astype(BF16),
                           m_ref, l_ref, acc_ref)
        wkn = wknew_ref[...]
        ktn = _pad_rows(wkn[:, :LANES], pad_ref, True)
        vtn = _pad_rows(wkn[:, LANES:], pad_ref, True)
        _flash_update_rows(tiled(_dot(qs_ref[...].astype(BF16), ktn), new_tile), vtn, m_ref, l_ref, acc_ref)
        o_w = acc_ref[...] * (1.0 / l_ref[...])
        o_c = oc_ref[...]
        for h in range(nh):
            sl = slice(h * rq, (h + 1) * rq)
            gate = lambda j: ng[:, MISC_NG + 3 * h + j:MISC_NG + 3 * h + j + 1]
            o_ref[:, h * LANES:(h + 1) * LANES] = gate(0) * o_c[sl] + gate(1) * o_s[sl] + gate(2) * o_w[sl]


def _sample_block_expander(n_pages, nb):
    key_blk = np.arange(n_pages * PAGE) // SEL_BLOCK
    e = (np.arange(nb)[:, None] == key_blk[None, :]).astype(np.float32)
    return jnp.asarray(e.reshape(nb, n_pages // PAGES_PER_STEP, STEP_KEYS).transpose(1, 0, 2), BF16)


def _nsa_sample(m, sknew, wknew, ck, cv, cache_sel, win_t, page_table, scmp, slast, snew, swin, db):
    n_pages = page_table.shape[1]
    ns = n_pages // PAGES_PER_STEP
    n_cmp = ck.shape[1]
    n_sblk = n_cmp // 2 + 1
    n_sel = min(N_SEL, n_sblk)
    nb = -(-n_sblk // LANES) * LANES
    assert win_t.shape[2] == WINDOW
    row = lambda width: pl.BlockSpec((DS_PAD, width), lambda b, s, pt: (b, 0))
    seq = lambda r, width: pl.BlockSpec((1, r, width), lambda b, s, pt: (b, 0, 0))
    const = lambda shape: pl.BlockSpec(shape, lambda b, s, pt: (0,) * len(shape))
    rows = MIX_HEADS * DS_PAD
    grid_spec = pltpu.PrefetchScalarGridSpec(
        num_scalar_prefetch=1,
        grid=(db, ns),
        in_specs=[row(1024), row(LANES), seq(n_cmp, LANES), seq(n_cmp, LANES), seq(2 * LANES, WINDOW),
                  row(2 * LANES), row(2 * LANES)]
        + _page_specs((2 * LANES, PAGE), lambda s: s)
        + [const((MIX_HEADS, DS_PAD, n_cmp)), const((MIX_HEADS, DS_PAD, PAGE)),
           const((MIX_HEADS, DS_PAD, PAGE)), const((MIX_HEADS, DS_PAD, WINDOW)), const((ns, nb, STEP_KEYS))],
        out_specs=row(1024),
        scratch_shapes=[pltpu.VMEM((rows, LANES), F32), pltpu.VMEM((N_GROUPS, DS_PAD, nb), F32),
                        pltpu.VMEM((rows, LANES), F32), pltpu.VMEM((PAGE, LANES), F32),
                        pltpu.VMEM((rows, 1), F32), pltpu.VMEM((rows, 1), F32),
                        pltpu.VMEM((rows, LANES), F32)])
    return pl.pallas_call(
        functools.partial(_nsa_sample_kernel, n_pages=n_pages, n_sel=n_sel),
        grid_spec=grid_spec,
        out_shape=jax.ShapeDtypeStruct((db * DS_PAD, 1024), F32),
        compiler_params=_cparams("arbitrary", "arbitrary"),
        name="nsa_sample",
    )(page_table, m["qn"], m["ng"], ck, cv, win_t, sknew, wknew,
      *([_feature_major_pages(cache_sel)] * PAGES_PER_STEP), scmp, slast, snew, swin,
      _sample_block_expander(n_pages, nb))


def _kv_leaf(t):
    nb, _, s = t.shape
    return t.reshape(nb, 2, N_GROUPS, HEAD_DIM, s).transpose(0, 4, 1, 2, 3)


def kernel(x_prompt, x_sample, cache_dsa_kv, cache_dsa_idx_k, cache_nsa_cmp_kv, cache_nsa_sel_kv,
           state_nsa_win_kv, page_table, rel_bias, ffn1_norm, ffn1_w_gu, ffn1_w_down, mix_norm, w_in,
           dsa_q_norm, dsa_k_norm, idx_k_norm, nsa_q_norm, nsa_cmp_k_norm, nsa_sel_k_norm, nsa_win_k_norm,
           nsa_w_phi, w_dsa_o, w_nsa_o, w_out, ffn2_norm, ffn2_w_gu, ffn2_w_down):
    b, s, d = x_prompt.shape
    db, ds, _ = x_sample.shape
    depth = ffn1_norm.shape[0]
    n_pages = page_table.shape[1]
    past = n_pages * PAGE
    assert s % TQ == 0 and ds <= DS_PAD and ds < CMP_BLOCK and past >= WINDOW
    n_p, n_s = b * s, db * DS_PAD
    tm_p, tm_s = min(512, s), min(512, n_s)
    tm_ffn = min(1024, n_p)

    hp = x_prompt.reshape(n_p, d)
    hs = jnp.pad(x_sample, ((0, 0), (0, DS_PAD - ds), (0, 0))).reshape(n_s, d)
    dsa_h = slice(0, MIX_HEADS)
    nsa_h = slice(MIX_HEADS, 2 * MIX_HEADS)
    tdiag, tprev, slast, snew, swin, scmp = _bias_tiles(rel_bias, past // CMP_BLOCK)
    tcmp = _cmp_tiles(rel_bias, MIX_HEADS, s // TQ, s // CMP_BLOCK)

    outs_p = [[] for _ in range(5)]
    outs_s = [[] for _ in range(5)]
    for l in range(depth):
        pw = _proj_weights(w_in[l], dsa_q_norm[l], dsa_k_norm[l], idx_k_norm[l], nsa_q_norm[l],
                           nsa_sel_k_norm[l], nsa_win_k_norm[l])
        cw = _compress_weights(nsa_w_phi[l], nsa_cmp_k_norm[l])
        wa, wb = _pad_head_rows(w_dsa_o[l]), _pad_head_rows(w_nsa_o[l])
        wo = w_out[l].astype(BF16)

        hp = _ffn(hp, ffn1_norm[l], ffn1_w_gu[l], ffn1_w_down[l], tm_ffn)
        mp = _proj(hp, mix_norm[l], pw, b, tm_p, BF16, True)
        oa = _dsa_prompt(mp, tdiag[dsa_h], tprev[dsa_h], b, s)
        ck, cvt = _compress_prompt(mp["ckv"], cw, b, s)
        ob = _nsa_prompt(mp, ck, cvt, tcmp, tdiag[nsa_h], tprev[nsa_h], b, s)
        hp = _merge(hp, oa.reshape(n_p, -1), ob.reshape(n_p, -1), mp["ga"], mp["gb"], wa, wb, wo, tm_p)
        hp = _ffn(hp, ffn2_norm[l], ffn2_w_gu[l], ffn2_w_down[l], tm_ffn)
        win_len = min(WINDOW, s)
        for lst, val in zip(outs_p, (_kv_leaf(mp["dkvt"]), mp["ikt"].transpose(0, 2, 1), _kv_leaf(mp["ckvt"]),
                                     _kv_leaf(mp["skvt"]), _kv_leaf(mp["wkvt"][:, :, s - win_len:]))):
            lst.append(val)

        hs = _ffn(hs, ffn1_norm[l], ffn1_w_gu[l], ffn1_w_down[l], tm_s)
        ms = _proj(hs, mix_norm[l], pw, 1, tm_s, F32, False)
        new_rows = lambda name: ms[name][0].T
        new_t = lambda name: ms[name][0].reshape(-1, db, DS_PAD)[:, :, :ds].transpose(1, 0, 2)
        oa_s = _dsa_sample(ms, new_rows("dkvt"), cache_dsa_idx_k[l], cache_dsa_kv[l], page_table,
                           slast[dsa_h], snew[dsa_h], db)
        ck_s, cv_s = _compress_sample(cache_nsa_cmp_kv[l], page_table, cw, db)
        win_t = _feature_major_pages(state_nsa_win_kv[l])
        ob_s = _nsa_sample(ms, new_rows("skvt"), new_rows("wkvt"), ck_s, cv_s, cache_nsa_sel_kv[l], win_t,
                           page_table, scmp[nsa_h], slast[nsa_h], snew[nsa_h], swin[nsa_h], db)
        hs = _merge(hs, oa_s.astype(BF16), ob_s.astype(BF16), ms["ga"], ms["gb"], wa, wb, wo, tm_s)
        hs = _ffn(hs, ffn2_norm[l], ffn2_w_gu[l], ffn2_w_down[l], tm_s)
        win_new = jnp.concatenate([win_t, new_t("wkvt")], axis=2)
        win_new = win_new[:, :, win_new.shape[2] - min(WINDOW, past + ds):]
        for lst, val in zip(outs_s, (_kv_leaf(new_t("dkvt")), new_t("ikt").transpose(0, 2, 1),
                                     _kv_leaf(new_t("ckvt")), _kv_leaf(new_t("skvt")), _kv_leaf(win_new))):
            lst.append(val)

    y_p = hp.reshape(b, s, d)
    y_s = hs.reshape(db, DS_PAD, d)[:, :ds]
    return (y_p, y_s) + tuple(jnp.stack(o) for o in outs_p) + tuple(jnp.stack(o) for o in outs_s)
```

```python
import functools
import math

import numpy as np
import jax
import jax.numpy as jnp
from jax import lax
from jax.experimental import pallas as pl
from jax.experimental.pallas import tpu as pltpu

F32 = jnp.float32
BF16 = jnp.bfloat16
I32 = jnp.int32

HEAD_DIM = 64
LANES = 128
SUBLANES = 8
N_GROUPS = 2
GROUP_HEADS = 4
MIX_HEADS = N_GROUPS * GROUP_HEADS
IDX_HEADS = 8
IDX_TOPK = 256
CMP_BLOCK = 32
SEL_BLOCK = 64
N_SEL = 16
WINDOW = 512
NUM_BUCKETS = 32
MAX_EXACT = 16
MAX_DISTANCE = 128
PAGE = 128
EPS = 1e-6
NEG = -1e30
BIG = 1e30
TQ = 256
DS_PAD = 8
PAGES_PER_STEP = 16
VMEM_LIMIT = 56 * 1024 * 1024

INT_MIN = -(2 ** 31)
I16 = jnp.int16
I16_MIN, I16_MAX = -(2 ** 15), 2 ** 15 - 1
PACKED_ROWS = 2 * SUBLANES
LOG2E = math.log2(math.e)


def _bucket_thresholds():
    n = np.arange(MAX_EXACT, 4 * MAX_DISTANCE)
    ratio = np.log(n.astype(np.float32) / np.float32(MAX_EXACT)) / np.float32(math.log(MAX_DISTANCE / MAX_EXACT))
    big = np.minimum(MAX_EXACT + (ratio * np.float32(NUM_BUCKETS - MAX_EXACT)).astype(np.int32), NUM_BUCKETS - 1)
    return [int(n[np.argmax(big >= b)]) for b in range(MAX_EXACT + 1, NUM_BUCKETS)]


BUCKET_THR = _bucket_thresholds()


def _dot(a, b):
    return jnp.dot(a, b, preferred_element_type=F32)


def _dot_nt(a, b):
    return lax.dot_general(a, b, (((1,), (1,)), ((), ())), preferred_element_type=F32)


def _cparams(*sem):
    return pltpu.CompilerParams(dimension_semantics=sem, vmem_limit_bytes=VMEM_LIMIT)


def _full(shape):
    nd = len(shape)
    return pl.BlockSpec(shape, lambda *_: (0,) * nd)


def _resident(shape):
    nd = len(shape)
    return pl.BlockSpec(shape, lambda *_: (0,) * nd, pipeline_mode=pl.Buffered(1))


def _iota2(shape, axis):
    return lax.broadcasted_iota(I32, shape, axis)


def _rms_rows(x, g):
    return x * lax.rsqrt(jnp.mean(x * x, axis=-1, keepdims=True) + EPS) * g


def _norm64(z, gain, s64):
    outs = []
    for j in range(z.shape[1] // LANES):
        zb = z[:, j * LANES:(j + 1) * LANES]
        sq = zb * zb
        hi = sq.astype(BF16)
        lo = (sq - hi.astype(F32)).astype(BF16)
        ms = (_dot(hi, s64) + _dot(lo, s64)) * (1.0 / HEAD_DIM)
        outs.append(zb * lax.rsqrt(ms + EPS) * gain[:, j * LANES:(j + 1) * LANES])
    return outs[0] if len(outs) == 1 else jnp.concatenate(outs, axis=1)


def _seg_matrix():
    r = np.arange(LANES) // HEAD_DIM
    return jnp.asarray((r[:, None] == r[None, :]).astype(np.float32), BF16)


def _ffn_kernel(x_ref, g_ref, wg_ref, wu_ref, wd_ref, o_ref, xn_ref, acc_ref):
    x = x_ref[...]
    xn_ref[...] = _rms_rows(x, g_ref[...]).astype(BF16)
    acc_ref[...] = jnp.zeros_like(acc_ref)

    def body(c, carry):
        xn = xn_ref[...]
        a = _dot(xn, wg_ref[c])
        b = _dot(xn, wu_ref[c])
        act = (a * jax.nn.sigmoid(a) * b).astype(BF16)
        acc_ref[...] += _dot(act, wd_ref[c])
        return carry

    lax.fori_loop(0, wg_ref.shape[0], body, 0)
    o_ref[...] = x + 0.5 * acc_ref[...]


def _ffn(x, g, w_gu, w_dn, tm):
    n, d = x.shape
    d_ff = w_dn.shape[0]
    fc = 256
    nc = d_ff // fc
    assert nc * fc == d_ff and n % tm == 0
    wg = w_gu[:, :d_ff].reshape(d, nc, fc).transpose(1, 0, 2).astype(BF16)
    wu = w_gu[:, d_ff:].reshape(d, nc, fc).transpose(1, 0, 2).astype(BF16)
    wd = w_dn.reshape(nc, fc, d).astype(BF16)
    return pl.pallas_call(
        _ffn_kernel,
        grid=(n // tm,),
        in_specs=[pl.BlockSpec((tm, d), lambda i: (i, 0)), _full((1, d)),
                  _resident((nc, d, fc)), _resident((nc, d, fc)), _resident((nc, fc, d))],
        out_specs=pl.BlockSpec((tm, d), lambda i: (i, 0)),
        out_shape=jax.ShapeDtypeStruct((n, d), F32),
        scratch_shapes=[pltpu.VMEM((tm, d), BF16), pltpu.VMEM((tm, d), F32)],
        compiler_params=_cparams("arbitrary"),
        name="ffn",
    )(x, g.reshape(1, d), wg, wu, wd)


_W_QD, _W_IQ, _W_QN = 0, 1024, 2048
_W_DKV, _W_CKV, _W_SKV, _W_WKV = 3072, 3328, 3584, 3840
_W_MG, _W_IK, _W_MISC, _W_TOTAL = 4096, 6144, 6272, 6400
MISC_IW = 0
MISC_NG = 8


def _proj_kernel(h_ref, gm_ref, w_ref, gqd_ref, gqn_ref, gk_ref, gik_ref, s64_ref, *out_refs, with_vt):
    refs = list(out_refs)
    hn_ref = refs.pop()
    qd_ref, iq_ref, qn_ref = refs[:3]
    kv_refs = refs[3:3 + (4 if with_vt else 3) * 3]
    ckv_ref, ckvt_ref, ga_ref, gb_ref, ikt_ref, ika_ref, iw_ref, ng_ref = refs[len(refs) - 8:]
    hn_ref[...] = _rms_rows(h_ref[...], gm_ref[...]).astype(BF16)
    s64 = s64_ref[...]
    tm = h_ref.shape[0]

    def seg(a, b):
        return _dot(hn_ref[...], w_ref[:, a:b])

    qd_ref[...] = _norm64(seg(_W_QD, _W_IQ), gqd_ref[...], s64).astype(qd_ref.dtype)
    iq_ref[...] = (seg(_W_IQ, _W_QN) * (HEAD_DIM ** -0.5)).astype(iq_ref.dtype)
    qn_ref[...] = _norm64(seg(_W_QN, _W_DKV), gqn_ref[...], s64).astype(qn_ref.dtype)
    per = 4 if with_vt else 3
    for j, a in enumerate((_W_DKV, _W_SKV, _W_WKV)):
        kvt_ref, k_ref, v_ref = kv_refs[j * per:j * per + 3]
        z = seg(a, a + 2 * LANES)
        kn = _norm64(z[:, :LANES], gk_ref[:, j * LANES:(j + 1) * LANES], s64)
        vt = z[:, LANES:].T
        kvt_ref[0, :LANES, :] = kn.T
        kvt_ref[0, LANES:, :] = vt
        k_ref[...] = kn.astype(BF16)
        v_ref[...] = z[:, LANES:].astype(BF16)
        if with_vt:
            vt_ref = kv_refs[j * per + 3]
            for cc in range(tm // TQ):
                vt_ref[0, cc] = vt[:, cc * TQ:(cc + 1) * TQ].astype(BF16)
    zc = seg(_W_CKV, _W_SKV)
    ckv_ref[...] = zc
    ckvt_ref[0, :LANES, :] = zc[:, :LANES].T
    ckvt_ref[0, LANES:, :] = zc[:, LANES:].T
    half = (_W_IK - _W_MG) // 2
    ga_ref[...] = jax.nn.sigmoid(seg(_W_MG, _W_MG + half))
    gb_ref[...] = jax.nn.sigmoid(seg(_W_MG + half, _W_IK))
    ikn = _norm64(seg(_W_IK, _W_MISC), gik_ref[...], s64)
    ikt_ref[0] = ikn.T[:HEAD_DIM, :]
    ika_ref[...] = ikn.astype(BF16)
    misc = seg(_W_MISC, _W_TOTAL)
    iw_ref[...] = misc * (IDX_HEADS ** -0.5)
    ng_ref[...] = jax.nn.sigmoid(misc)


def _pad_heads(w, by_group):
    d = w.shape[0]
    w = w.reshape(d, MIX_HEADS, HEAD_DIM)
    z = jnp.zeros_like(w)
    if by_group:
        grp = (jnp.arange(MIX_HEADS) // GROUP_HEADS)[None, :, None]
        lo, hi = jnp.where(grp == 0, w, z), jnp.where(grp == 1, w, z)
    else:
        lo, hi = w, z
    return jnp.concatenate([lo, hi], axis=-1).reshape(d, MIX_HEADS * LANES)


def _proj_weights(w_in, dsa_q_norm, dsa_k_norm, idx_k_norm, nsa_q_norm, nsa_sel_k_norm, nsa_win_k_norm):
    d = w_in.shape[0]
    sizes = (512, 256, 512, 8, 64, 512, 256, 256, 256, 24, 2 * d)
    offs = np.concatenate([[0], np.cumsum(sizes)])
    dq, dkv, iq, iw, ik, nq, ckv, skv, wkv, ng, mg = [w_in[:, offs[j]:offs[j + 1]] for j in range(len(sizes))]
    misc = jnp.concatenate([iw, ng, jnp.zeros((d, LANES - 32), F32)], axis=1)
    ikp = jnp.concatenate([ik, jnp.zeros((d, LANES - HEAD_DIM), F32)], axis=1)
    w = jnp.concatenate([_pad_heads(dq, True), _pad_heads(iq, False), _pad_heads(nq, True),
                         dkv, ckv, skv, wkv, mg, ikp, misc], axis=1).astype(BF16)
    assert w.shape[1] == _W_TOTAL
    scale = HEAD_DIM ** -0.5 * LOG2E
    gqd = _pad_heads(jnp.tile(dsa_q_norm * scale, MIX_HEADS)[None, :], True)
    gqn = _pad_heads(jnp.tile(nsa_q_norm * scale, MIX_HEADS)[None, :], True)
    gk = jnp.concatenate([jnp.tile(g, 2) for g in (dsa_k_norm, nsa_sel_k_norm, nsa_win_k_norm)])[None, :]
    gik = jnp.concatenate([idx_k_norm, jnp.zeros((LANES - HEAD_DIM,), F32)])[None, :]
    return w, gqd, gqn, gk, gik


def _proj(h, g_mix, pw, nb, tm, q_dtype, with_vt):
    n, d = h.shape
    s = n // nb
    nt = s // tm
    w, gqd, gqn, gk, gik = pw
    row = lambda width: pl.BlockSpec((tm, width), lambda b, j: (b * nt + j, 0))
    colmaj = lambda width: pl.BlockSpec((1, width, tm), lambda b, j: (b, 0, j))
    const = lambda shape: pl.BlockSpec(shape, lambda b, j: (0,) * len(shape))
    tok = lambda width, dt: (row(width), jax.ShapeDtypeStruct((n, width), dt))
    feat = lambda width: (colmaj(width), jax.ShapeDtypeStruct((nb, width, s), F32))
    outs = [("qd", tok(1024, q_dtype)), ("iq", tok(1024, q_dtype)), ("qn", tok(1024, q_dtype))]
    for p in ("d", "s", "w"):
        outs += [(p + "kvt", feat(256)), (p + "k", tok(LANES, BF16)), (p + "v", tok(LANES, BF16))]
        if with_vt:
            outs.append((p + "vt", (pl.BlockSpec((1, tm // TQ, LANES, TQ), lambda b, j: (b, j, 0, 0)),
                                    jax.ShapeDtypeStruct((nb, s // TQ, LANES, TQ), BF16))))
    outs += [("ckv", tok(256, F32)), ("ckvt", feat(256)), ("ga", tok(d, F32)), ("gb", tok(d, F32)),
             ("ikt", feat(HEAD_DIM)), ("ika", tok(LANES, BF16)), ("iw", tok(LANES, F32)), ("ng", tok(LANES, F32))]
    res = pl.pallas_call(
        functools.partial(_proj_kernel, with_vt=with_vt),
        grid=(nb, nt),
        in_specs=[row(d), const((1, d)), const(w.shape), const(gqd.shape), const(gqn.shape),
                  const(gk.shape), const(gik.shape), const((LANES, LANES))],
        out_specs=[spec for _, (spec, _) in outs],
        out_shape=[shape for _, (_, shape) in outs],
        scratch_shapes=[pltpu.VMEM((tm, d), BF16)],
        compiler_params=_cparams("arbitrary", "arbitrary"),
        name="proj",
    )(h, g_mix.reshape(1, d), w, gqd, gqn, gk, gik, _seg_matrix())
    return {name: r for (name, _), r in zip(outs, res)}


def _bias_tile(dist, valid, tab_ref, h):
    n = jnp.maximum(dist, 0)
    big = jnp.full(n.shape, MAX_EXACT, I32)
    for thr in BUCKET_THR:
        big = big + (n >= thr).astype(I32)
    bucket = jnp.where(n < MAX_EXACT, n, big)
    far = tab_ref[NUM_BUCKETS - 1, h]
    out = jnp.zeros(n.shape, F32)
    for b in range(NUM_BUCKETS - 1):
        out = jnp.where(bucket == b, (tab_ref[b, h] - far) * LOG2E, out)
    return jnp.where(valid, out, NEG)


def _cmp_block_of(u, n_cmp):
    half = n_cmp // 2
    return 2 * jnp.where(u >= half, u - half, u) + (u >= half).astype(I32)


def _tiles_kernel(tab_ref, diag_ref, prev_ref, slast_ref, snew_ref, swin_ref, scmp_ref, *, n_cmp):
    h = pl.program_id(0)
    kr = _iota2((TQ, TQ), 0)
    qc = _iota2((TQ, TQ), 1)
    diag_ref[0] = _bias_tile(qc - kr, qc >= kr, tab_ref, h)
    prev_ref[0] = _bias_tile(TQ + qc - kr, qc >= -1, tab_ref, h)
    r8 = _iota2((DS_PAD, PAGE), 0)
    c8 = _iota2((DS_PAD, PAGE), 1)
    slast_ref[0] = _bias_tile(PAGE + r8 - c8, r8 >= -1, tab_ref, h)
    snew_ref[0] = _bias_tile(r8 - c8, r8 >= c8, tab_ref, h)
    rw = _iota2((DS_PAD, WINDOW), 0)
    cw = _iota2((DS_PAD, WINDOW), 1)
    swin_ref[0] = _bias_tile(WINDOW + rw - cw, cw > rw, tab_ref, h)
    rc = _iota2((DS_PAD, n_cmp), 0)
    jc = _cmp_block_of(_iota2((DS_PAD, n_cmp), 1), n_cmp)
    scmp_ref[0] = _bias_tile(n_cmp * CMP_BLOCK + rc - (jc * CMP_BLOCK + CMP_BLOCK - 1), rc >= -1, tab_ref, h)


def _bias_tiles(rel_bias, n_cmp_sample):
    nh = rel_bias.shape[1]
    per_head = lambda *s: pl.BlockSpec((1,) + s, lambda h: (h,) + (0,) * len(s))
    shapes = [(TQ, TQ), (TQ, TQ), (DS_PAD, PAGE), (DS_PAD, PAGE), (DS_PAD, WINDOW), (DS_PAD, n_cmp_sample)]
    return pl.pallas_call(
        functools.partial(_tiles_kernel, n_cmp=n_cmp_sample),
        grid=(nh,),
        in_specs=[pl.BlockSpec(memory_space=pltpu.SMEM)],
        out_specs=[per_head(*s) for s in shapes],
        out_shape=[jax.ShapeDtypeStruct((nh,) + s, F32) for s in shapes],
        compiler_params=_cparams("arbitrary"),
        name="bias_tiles",
    )(rel_bias)


def _cmp_tiles_kernel(tab_ref, o_ref, *, head0, n_cmp):
    h = pl.program_id(0) + head0
    i = pl.program_id(1)
    j = _cmp_block_of(_iota2((n_cmp, TQ), 0), n_cmp)
    dist = i * TQ + _iota2((n_cmp, TQ), 1) - (j * CMP_BLOCK + CMP_BLOCK - 1)
    o_ref[0, 0] = _bias_tile(dist, dist >= 0, tab_ref, h)


def _cmp_tiles(rel_bias, head0, n_qblk, n_cmp):
    return pl.pallas_call(
        functools.partial(_cmp_tiles_kernel, head0=head0, n_cmp=n_cmp),
        grid=(MIX_HEADS, n_qblk),
        in_specs=[pl.BlockSpec(memory_space=pltpu.SMEM)],
        out_specs=pl.BlockSpec((1, 1, n_cmp, TQ), lambda h, i: (i, h, 0, 0)),
        out_shape=jax.ShapeDtypeStruct((n_qblk, MIX_HEADS, n_cmp, TQ), F32),
        compiler_params=_cparams("arbitrary", "arbitrary"),
        name="cmp_tiles",
    )(rel_bias)


def _sortable(x):
    b = lax.bitcast_convert_type(x, I32)
    return jnp.where(x == 0.0, 0, b ^ ((b >> 31) & 0x7FFFFFFF))


def _bit_search(count_ge, shape, k, lowest, bits):
    zero = jnp.zeros(shape, I32)
    t0 = jnp.where(count_ge(zero) >= k, zero, jnp.full(shape, lowest, I32))

    def bit_body(bi, t):
        cand = t + jnp.left_shift(jnp.int32(1), bits - 2 - bi)
        return jnp.where(count_ge(cand) >= k, cand, t)

    return lax.fori_loop(0, bits - 1, bit_body, t0)


def _kth_largest(count_ge, shape, k):
    return _bit_search(count_ge, shape, k, INT_MIN, 32)


def _kth_largest16(count_ge16, k):
    return _bit_search(count_ge16, (1, TQ), k, I16_MIN, 16)


def _fold_rows(x):
    return x.reshape(x.shape[0] // SUBLANES, SUBLANES, x.shape[1]).sum(axis=0)


def _fold_lanes(x):
    out = x[:, :LANES]
    for j in range(1, x.shape[1] // LANES):
        out = out + x[:, j * LANES:(j + 1) * LANES]
    return out


def _rep_lanes(x, width):
    n = width // LANES
    return x if n == 1 else jnp.concatenate([x] * n, axis=1)


def _stack_heads(src, dst_ref, heads, rows):
    for h in range(heads):
        dst_ref[h * rows:(h + 1) * rows, :] = src[:, h * LANES:(h + 1) * LANES].astype(dst_ref.dtype)


def _tri(n, lower):
    r = np.arange(n)
    m = (r[None, :] < r[:, None]) if lower else (r[:, None] < r[None, :])
    return jnp.asarray(m.astype(np.float32), BF16)


def _select_bias_t(keys, thr, need, run_ref, ltri):
    eq = keys == thr
    eq_f = jnp.where(eq, 1.0, 0.0)
    before = _dot(ltri, eq_f.astype(BF16)) + run_ref[...]
    tie_ok = before < need
    run_ref[...] += jnp.sum(eq_f, axis=0, keepdims=True)
    return jnp.where(keys > thr, 0.0, jnp.where(eq, jnp.where(tie_ok, 0.0, NEG), NEG))


ACC_ROWS = LANES + 16


def _flash_init_t(m_ref, acc_ref):
    m_ref[...] = jnp.full(m_ref.shape, -3e38, F32)
    acc_ref[...] = jnp.zeros_like(acc_ref)


def _flash_update_t(lg, vt, m_ref, acc_ref):
    m_old = m_ref[...]
    m_new = jnp.maximum(m_old, jnp.max(lg, axis=0, keepdims=True))
    alpha = jnp.exp2(m_old - m_new)
    p = jnp.exp2(lg - m_new).astype(BF16)
    vta = jnp.concatenate([vt, jnp.ones((ACC_ROWS - LANES, vt.shape[1]), BF16)], axis=0)
    acc_ref[...] = alpha * acc_ref[...] + _dot(vta, p)
    m_ref[...] = m_new


def _flash_result_t(acc_ref):
    return acc_ref[:LANES, :] * (1.0 / acc_ref[LANES:LANES + 1, :])


def _head_logits(kc, qs_ref, heads, add_of):
    lg = _dot_nt(kc, qs_ref[...])
    return jnp.concatenate([lg[:, r * TQ:(r + 1) * TQ] + add_of(r) for r in range(heads)], axis=1)


def _key_rows(ref, c, n=1):
    return ref[0, pl.ds(pl.multiple_of(c * TQ, TQ), n * TQ), :]


def _value_cols(vt_ref, c, n=1):
    return vt_ref[0, c] if n == 1 else jnp.concatenate([vt_ref[0, c + j] for j in range(n)], axis=1)


def _dsa_prompt_kernel(qa_ref, iqa_ref, iw_ref, ika_ref, k_ref, vt_ref, tdiag_ref, tprev_ref, ltri_ref,
                       o_ref, keys_ref, half_ref, qs_ref, iqs_ref, wt_ref, run_ref, m_ref, acc_ref, *, k_top):
    i = pl.program_id(1)
    nh = MIX_HEADS
    _stack_heads(qa_ref[0], qs_ref, nh, TQ)
    _stack_heads(iqa_ref[0], iqs_ref, IDX_HEADS, TQ)
    wt_ref[...] = iw_ref[0].T

    def score_chunk(c, causal):
        s = _dot_nt(_key_rows(ika_ref, c), iqs_ref[...])
        sc = jnp.zeros((TQ, TQ), F32)
        for h in range(IDX_HEADS):
            sc = sc + jnp.maximum(s[:, h * TQ:(h + 1) * TQ], 0.0) * wt_ref[MISC_IW + h:MISC_IW + h + 1, :]
        if causal:
            sc = jnp.where(_iota2((TQ, TQ), 0) <= _iota2((TQ, TQ), 1), sc, NEG)
        key = _sortable(sc)
        keys_ref[c] = key
        half_ref[c] = (key >> 16).astype(I16)

    def score_body(c, carry):
        score_chunk(c, False)
        return carry

    lax.fori_loop(0, i, score_body, 0)
    score_chunk(i, True)

    def count_ge(cand):
        def body(c, acc):
            return acc + _fold_rows((keys_ref[c] >= cand).astype(I32))

        acc = lax.fori_loop(0, i + 1, body, jnp.zeros((SUBLANES, TQ), I32))
        return jnp.sum(acc, axis=0, keepdims=True)

    def count_ge16(cand):
        c16 = cand.astype(I16)

        def body(c, acc):
            hit = jnp.where(half_ref[c] >= c16, jnp.int16(1), jnp.int16(0))
            for j in range(TQ // PACKED_ROWS):
                acc = acc + hit[j * PACKED_ROWS:(j + 1) * PACKED_ROWS]
            return acc

        acc = lax.fori_loop(0, i + 1, body, jnp.zeros((PACKED_ROWS, TQ), I16))
        return jnp.sum(acc.astype(I32), axis=0, keepdims=True)

    hi = _kth_largest16(count_ge16, k_top)
    above = jnp.where(hi < I16_MAX, count_ge16(jnp.minimum(hi + 1, I16_MAX)), 0)

    def low_body(c, carry):
        key = keys_ref[c]
        low = (key & 0xFFFF) + I16_MIN
        half_ref[c] = jnp.where((key >> 16) == hi, low, I16_MIN).astype(I16)
        return carry

    lax.fori_loop(0, i + 1, low_body, 0)
    thr = hi * 65536 + (_kth_largest16(count_ge16, k_top - above) - I16_MIN)
    need = (k_top - count_ge(thr + 1)).astype(F32)
    has_tie = jnp.max(count_ge(thr)) > k_top

    _flash_init_t(m_ref, acc_ref)
    run_ref[...] = jnp.zeros_like(run_ref)

    def attend(c, tile_ref, n=1):
        sels = []
        for j in range(n):
            keys = keys_ref[c + j]
            sels.append(lax.cond(has_tie,
                                 lambda keys=keys: _select_bias_t(keys, thr, need, run_ref, ltri_ref[...]),
                                 lambda keys=keys: jnp.where(keys >= thr, 0.0, NEG)))
        sel = sels[0] if n == 1 else jnp.concatenate(sels, axis=0)
        add_of = (lambda h: sel) if tile_ref is None else (lambda h: sel + tile_ref[h])
        lg = _head_logits(_key_rows(k_ref, c, n), qs_ref, nh, add_of)
        _flash_update_t(lg, _value_cols(vt_ref, c, n), m_ref, acc_ref)

    n_far = jnp.maximum(i - 1, 0)

    def far_body(p, carry):
        attend(2 * p, None, 2)
        return carry

    lax.fori_loop(0, n_far // 2, far_body, 0)

    @pl.when(n_far % 2 == 1)
    def _():
        attend(n_far - 1, None)

    @pl.when(i >= 1)
    def _():
        attend(i - 1, tprev_ref)

    attend(i, tdiag_ref)
    ot = _flash_result_t(acc_ref)
    for h in range(nh):
        o_ref[0, :, h * LANES:(h + 1) * LANES] = ot[:, h * TQ:(h + 1) * TQ].T.astype(o_ref.dtype)


def _dsa_prompt(m, tdiag, tprev, b, s):
    nq = s // TQ
    k_top = min(IDX_TOPK, s // 4)
    assert TQ >= k_top
    qblk = lambda width: pl.BlockSpec((1, TQ, width), lambda bi, i: (bi, i, 0))
    seq = lambda width: pl.BlockSpec((1, s, width), lambda bi, i: (bi, 0, 0))
    r3 = lambda a: a.reshape(b, s, a.shape[-1])
    cols = MIX_HEADS * TQ
    return pl.pallas_call(
        functools.partial(_dsa_prompt_kernel, k_top=k_top),
        grid=(b, nq),
        in_specs=[qblk(1024), qblk(1024), qblk(LANES), seq(LANES), seq(LANES),
                  pl.BlockSpec((1, nq, LANES, TQ), lambda bi, i: (bi, 0, 0, 0)),
                  _full((MIX_HEADS, TQ, TQ)), _full((MIX_HEADS, TQ, TQ)), _full((TQ, TQ))],
        out_specs=qblk(1024),
        out_shape=jax.ShapeDtypeStruct((b, s, MIX_HEADS * LANES), BF16),
        scratch_shapes=[pltpu.VMEM((nq, TQ, TQ), I32),
                        pltpu.VMEM((nq, TQ, TQ), I16),
                        pltpu.VMEM((cols, LANES), BF16),
                        pltpu.VMEM((IDX_HEADS * TQ, LANES), BF16),
                        pltpu.VMEM((LANES, TQ), F32),
                        pltpu.VMEM((1, TQ), F32),
                        pltpu.VMEM((1, cols), F32),
                        pltpu.VMEM((ACC_ROWS, cols), F32)],
        compiler_params=_cparams("arbitrary", "arbitrary"),
        name="dsa_prompt",
    )(r3(m["qd"]), r3(m["iq"]), r3(m["iw"]), r3(m["ika"]), r3(m["dk"]), m["dvt"],
      tdiag, tprev, _tri(TQ, True))


def _compress_weights(w_phi, g_ck):
    eye = jnp.eye(N_GROUPS, dtype=F32)
    wbig = jnp.einsum("clde,cx,gy->lcgdxye", w_phi, jnp.eye(2, dtype=F32), eye)
    wbig = wbig.reshape(CMP_BLOCK * 4 * HEAD_DIM, 4 * HEAD_DIM).astype(BF16)
    return wbig, jnp.tile(g_ck, 2)[None, :]


def _compress_kernel(x_ref, wbig_ref, gck_ref, s64_ref, ck_ref, cvt_ref):
    kdim = wbig_ref.shape[0]
    half = x_ref.shape[1]
    for p in range(2):
        c = _dot(x_ref[0, :, p * kdim:(p + 1) * kdim].astype(BF16), wbig_ref[...])
        ck_ref[0, p * half:(p + 1) * half, :] = _norm64(c[:, :LANES], gck_ref[...], s64_ref[...]).astype(BF16)
        cvt_ref[0, :, p * half:(p + 1) * half] = c[:, LANES:].T.astype(BF16)


def _compress_prompt(ckv, cw, b, s):
    wbig, gck = cw
    nc = s // CMP_BLOCK
    kdim = wbig.shape[0]
    x = ckv.reshape(b, nc // 2, 2 * kdim)
    return pl.pallas_call(
        _compress_kernel,
        grid=(b,),
        in_specs=[pl.BlockSpec((1, nc // 2, 2 * kdim), lambda bi: (bi, 0, 0)), _full(wbig.shape),
                  _full((1, LANES)), _full((LANES, LANES))],
        out_specs=[pl.BlockSpec((1, nc, LANES), lambda bi: (bi, 0, 0)),
                   pl.BlockSpec((1, LANES, nc), lambda bi: (bi, 0, 0))],
        out_shape=[jax.ShapeDtypeStruct((b, nc, LANES), BF16), jax.ShapeDtypeStruct((b, LANES, nc), BF16)],
        compiler_params=_cparams("arbitrary"),
        name="compress_prompt",
    )(x, wbig, gck, _seg_matrix())


def _pick_blocks(score, n_sel, axis):
    pos = _iota2(score.shape, axis)
    picked = jnp.zeros(score.shape, F32)
    for _ in range(n_sel):
        top = jnp.max(score, axis=axis, keepdims=True)
        first = jnp.min(jnp.where(score == top, pos, score.shape[axis]), axis=axis, keepdims=True)
        hit = pos == first
        picked = jnp.where(hit, 1.0, picked)
        score = jnp.where(hit, -jnp.inf, score)
    return picked


def _softmax_valid(lg, axis):
    valid = lg > 0.5 * NEG
    p = jnp.where(valid, jnp.exp2(lg - jnp.max(lg, axis=axis, keepdims=True)), 0.0)
    tot = jnp.sum(p, axis=axis, keepdims=True)
    return p * (1.0 / jnp.where(tot > 0.0, tot, 1.0))


def _nsa_prompt_kernel(qa_ref, ng_ref, ck_ref, cvt_ref, ks_ref, vst_ref, kw_ref, vwt_ref,
                       tcmp_ref, tdiag_ref, tprev_ref, tw2_ref, exp_ref,
                       o_ref, qs_ref, ngt_ref, oc_ref, os_ref, m_ref, acc_ref, *, n_sel):
    i = pl.program_id(1)
    gh = GROUP_HEADS
    nh = MIX_HEADS
    n_cmp = ck_ref.shape[1]
    n_blk = n_cmp // 2
    ngt_ref[...] = ng_ref[0].T
    blk = _iota2((n_blk, TQ), 0)
    cur = (i * TQ + _iota2((n_blk, TQ), 1)) // SEL_BLOCK
    forced = (blk == 0) | (blk == cur) | (blk == cur - 1)

    _stack_heads(qa_ref[0], qs_ref, nh, TQ)

    pc = _softmax_valid(_head_logits(ck_ref[0], qs_ref, nh, lambda h: tcmp_ref[0, h]), 0)
    oc_ref[...] = _dot(cvt_ref[0], pc.astype(BF16))
    scores = []
    for g in range(N_GROUPS):
        imp = pc[:, g * gh * TQ:(g * gh + 1) * TQ]
        for r in range(1, gh):
            imp = imp + pc[:, (g * gh + r) * TQ:(g * gh + r + 1) * TQ]
        imp = imp[:n_blk] + imp[n_blk:]
        scores.append(jnp.where(forced, BIG, jnp.where(blk <= cur, imp, NEG)))
    picked = _pick_blocks(jnp.concatenate(scores, axis=1), n_sel, 0).astype(BF16)

    def branch_chunk(k_ref, vt_ref, c, tile_ref, extra_of, n=1):
        if tile_ref is None:
            add_of = extra_of
        elif extra_of is None:
            add_of = lambda h: tile_ref[h]
        else:
            add_of = lambda h: tile_ref[h] + extra_of(h)
        lg = _head_logits(_key_rows(k_ref, c, n), qs_ref, nh, add_of)
        _flash_update_t(lg, _value_cols(vt_ref, c, n), m_ref, acc_ref)

    _flash_init_t(m_ref, acc_ref)

    def sel_chunk(c, tile_ref, n=1):
        keeps = [jnp.where(_dot(exp_ref[c + j], picked) > 0.5, 0.0, NEG) for j in range(n)]
        keep = keeps[0] if n == 1 else jnp.concatenate(keeps, axis=0)
        branch_chunk(ks_ref, vst_ref, c, tile_ref, lambda h: keep[:, (h // gh) * TQ:(h // gh + 1) * TQ], n)

    n_far = jnp.maximum(i - 1, 0)

    def far_body(p, carry):
        sel_chunk(2 * p, None, 2)
        return carry

    lax.fori_loop(0, n_far // 2, far_body, 0)

    @pl.when(n_far % 2 == 1)
    def _():
        sel_chunk(n_far - 1, None)

    @pl.when(i >= 1)
    def _():
        sel_chunk(i - 1, tprev_ref)

    sel_chunk(i, tdiag_ref)
    os_ref[...] = _flash_result_t(acc_ref)

    _flash_init_t(m_ref, acc_ref)

    @pl.when(i >= 2)
    def _():
        edge = tw2_ref[...]
        branch_chunk(kw_ref, vwt_ref, i - 2, None, lambda h: edge)

    @pl.when(i >= 1)
    def _():
        branch_chunk(kw_ref, vwt_ref, i - 1, tprev_ref, None)

    branch_chunk(kw_ref, vwt_ref, i, tdiag_ref, None)
    ow = _flash_result_t(acc_ref)
    for h in range(nh):
        cs = slice(h * TQ, (h + 1) * TQ)
        gate = lambda j: ngt_ref[MISC_NG + 3 * h + j:MISC_NG + 3 * h + j + 1, :]
        ot = gate(0) * oc_ref[:, cs] + gate(1) * os_ref[:, cs] + gate(2) * ow[:, cs]
        o_ref[0, :, h * LANES:(h + 1) * LANES] = ot.T.astype(o_ref.dtype)


def _block_expander_t(n_blk, n_keys):
    key_blk = np.arange(n_keys) // SEL_BLOCK
    e = (key_blk[:, None] == np.arange(n_blk)[None, :]).astype(np.float32)
    return jnp.asarray(e.reshape(n_keys // TQ, TQ, n_blk), BF16)


def _window_edge_tile_t():
    r = np.arange(TQ)
    return jnp.asarray(np.where(r[:, None] > r[None, :], 0.0, NEG).astype(np.float32))


def _nsa_prompt(m, ck, cvt, tcmp, tdiag, tprev, b, s):
    nq = s // TQ
    n_cmp = s // CMP_BLOCK
    n_blk = s // SEL_BLOCK
    n_sel = min(N_SEL, n_blk)
    assert WINDOW == 2 * TQ
    qblk = lambda width: pl.BlockSpec((1, TQ, width), lambda bi, i: (bi, i, 0))
    seq = lambda rows, width: pl.BlockSpec((1, rows, width), lambda bi, i: (bi, 0, 0))
    seqt = pl.BlockSpec((1, nq, LANES, TQ), lambda bi, i: (bi, 0, 0, 0))
    r3 = lambda a: a.reshape(b, s, a.shape[-1])
    cols = MIX_HEADS * TQ
    return pl.pallas_call(
        functools.partial(_nsa_prompt_kernel, n_sel=n_sel),
        grid=(b, nq),
        in_specs=[qblk(1024), qblk(LANES), seq(n_cmp, LANES), seq(LANES, n_cmp),
                  seq(s, LANES), seqt, seq(s, LANES), seqt,
                  pl.BlockSpec((1, MIX_HEADS, n_cmp, TQ), lambda bi, i: (i, 0, 0, 0)),
                  _full((MIX_HEADS, TQ, TQ)), _full((MIX_HEADS, TQ, TQ)), _full((TQ, TQ)),
                  _full((nq, TQ, n_blk))],
        out_specs=qblk(1024),
        out_shape=jax.ShapeDtypeStruct((b, s, MIX_HEADS * LANES), BF16),
        scratch_shapes=[pltpu.VMEM((cols, LANES), BF16),
                        pltpu.VMEM((LANES, TQ), F32),
                        pltpu.VMEM((LANES, cols), F32),
                        pltpu.VMEM((LANES, cols), F32),
                        pltpu.VMEM((1, cols), F32),
                        pltpu.VMEM((ACC_ROWS, cols), F32)],
        compiler_params=_cparams("arbitrary", "arbitrary"),
        name="nsa_prompt",
    )(r3(m["qn"]), r3(m["ng"]), ck, cvt, r3(m["sk"]), m["svt"], r3(m["wk"]), m["wvt"],
      tcmp, tdiag, tprev, _window_edge_tile_t(), _block_expander_t(n_blk, s))


def _merge_kernel(h_ref, oa_ref, ob_ref, ga_ref, gb_ref, wa_ref, wb_ref, wo_ref, o_ref):
    merged = ga_ref[...] * _dot(oa_ref[...], wa_ref[...]) + gb_ref[...] * _dot(ob_ref[...], wb_ref[...])
    o_ref[...] = h_ref[...] + _dot(merged.astype(BF16), wo_ref[...])


def _pad_head_rows(w):
    return _pad_heads(w.T, True).T.astype(BF16)


def _merge(h, oa, ob, ga, gb, wa, wb, wo, tm):
    n, d = h.shape
    row = lambda width: pl.BlockSpec((tm, width), lambda i: (i, 0))
    return pl.pallas_call(
        _merge_kernel,
        grid=(n // tm,),
        in_specs=[row(d), row(1024), row(1024), row(d), row(d),
                  _full(wa.shape), _full(wb.shape), _full(wo.shape)],
        out_specs=row(d),
        out_shape=jax.ShapeDtypeStruct((n, d), F32),
        compiler_params=_cparams("arbitrary"),
        name="merge",
    )(h, oa, ob, ga, gb, wa, wb, wo)


STEP_KEYS = PAGES_PER_STEP * PAGE


def _page_specs(shape_tail, step_of):
    nd = len(shape_tail)

    def spec(pp):
        def imap(b, s, pt):
            return (pt[b, step_of(s) * PAGES_PER_STEP + pp],) + (0,) * nd
        return pl.BlockSpec((1,) + shape_tail, imap)

    return [spec(pp) for pp in range(PAGES_PER_STEP)]


def _feature_major_pages(cache):
    n, rows = cache.shape[:2]
    nd = cache.ndim
    return cache.transpose((0,) + tuple(range(2, nd)) + (1,)).reshape(n, -1, rows)


def _cat_pages(refs, lo, hi):
    return jnp.concatenate([r[0, lo:hi, :].astype(BF16) for r in refs], axis=1)


def _pad_rows(x, pad_ref, transpose=False):
    pad_ref[...] = jnp.zeros_like(pad_ref)
    pad_ref[0:DS_PAD, :] = x
    full = pad_ref[...]
    return (full.T if transpose else full).astype(BF16)


def _select_bias_rows(keys, thr, need, run_ref, utri, ones):
    out = []
    for j in range(keys.shape[1] // PAGE):
        kj = keys[:, j * PAGE:(j + 1) * PAGE]
        eq = kj == thr
        eq_b = jnp.where(eq, 1.0, 0.0).astype(BF16)
        tie_ok = _dot(eq_b, utri) + run_ref[...] < need
        run_ref[...] += _dot(eq_b, ones)
        out.append(jnp.where(kj > thr, 0.0, jnp.where(eq, jnp.where(tie_ok, 0.0, NEG), NEG)))
    return out[0] if len(out) == 1 else jnp.concatenate(out, axis=1)


def _flash_init(m_ref, l_ref, acc_ref):
    m_ref[...] = jnp.full(m_ref.shape, -3e38, F32)
    l_ref[...] = jnp.zeros_like(l_ref)
    acc_ref[...] = jnp.zeros_like(acc_ref)


def _flash_update_rows(lg, vt, m_ref, l_ref, acc_ref):
    m_old = m_ref[...]
    m_new = jnp.maximum(m_old, jnp.max(lg, axis=-1, keepdims=True))
    alpha = jnp.exp2(m_old - m_new)
    p = jnp.exp2(lg - m_new)
    l_ref[...] = alpha * l_ref[...] + jnp.sum(p, axis=-1, keepdims=True)
    acc_ref[...] = alpha * acc_ref[...] + _dot_nt(p.astype(BF16), vt)
    m_ref[...] = m_new


def _last_page_tile(tile, on):
    t = jnp.where(on, tile, 0.0)
    if PAGES_PER_STEP == 1:
        return t
    return jnp.concatenate([jnp.zeros((DS_PAD, STEP_KEYS - PAGE), F32), t], axis=1)


def _dsa_sample_kernel(pt_ref, qa_ref, iqa_ref, iw_ref, iknew_ref, kvnew_ref, *rest, n_pages, k_top):
    npp = PAGES_PER_STEP
    idx_refs, kv_refs = rest[:npp], rest[npp:2 * npp]
    (slast_ref, snew_ref, utri_ref, ones_ref, o_ref, keys_ref, keysn_ref, qs_ref, iqs_ref, wb_ref,
     thr_ref, need_ref, run_ref, pad_ref, m_ref, l_ref, acc_ref) = rest[2 * npp:]
    s = pl.program_id(1)
    ns = n_pages // npp
    nh = MIX_HEADS
    rq = DS_PAD

    def score(sd):
        sc = jnp.zeros((rq, sd.shape[1]), F32)
        for h in range(IDX_HEADS):
            sc = sc + jnp.maximum(sd[h * rq:(h + 1) * rq], 0.0) * _rep_lanes(wb_ref[h], sd.shape[1])
        return sc

    @pl.when(s == 0)
    def _():
        _stack_heads(qa_ref[...], qs_ref, nh, rq)
        _stack_heads(iqa_ref[...], iqs_ref, IDX_HEADS, rq)
        iw = iw_ref[...]
        for h in range(IDX_HEADS):
            wb_ref[h] = jnp.broadcast_to(iw[:, MISC_IW + h:MISC_IW + h + 1], (rq, LANES))

    @pl.when(s < ns)
    def _():
        ikt = _cat_pages(idx_refs, 0, HEAD_DIM)
        keys_ref[s] = _sortable(score(_dot(iqs_ref[:, :HEAD_DIM].astype(BF16), ikt)))

    @pl.when(s == ns - 1)
    def _():
        sc = score(_dot_nt(iqs_ref[...].astype(BF16), _pad_rows(iknew_ref[...], pad_ref)))
        sc = jnp.where(_iota2((rq, PAGE), 0) >= _iota2((rq, PAGE), 1), sc, NEG)
        keysn_ref[...] = _sortable(sc)

        def count_ge(cand):
            cw = _rep_lanes(cand, STEP_KEYS)

            def body(c, acc):
                return acc + _fold_lanes((keys_ref[c] >= cw).astype(I32))

            acc = lax.fori_loop(0, ns, body, (keysn_ref[...] >= cand).astype(I32))
            return jnp.broadcast_to(jnp.sum(acc, axis=-1, keepdims=True), acc.shape)

        thr = _kth_largest(count_ge, (rq, LANES), k_top)
        thr_ref[...] = thr
        need_ref[...] = (k_top - count_ge(thr + 1)).astype(F32)
        _flash_init(m_ref, l_ref, acc_ref)
        run_ref[...] = jnp.zeros_like(run_ref)

    def attend(keys, kt, vt, tile_of):
        sel = _select_bias_rows(keys, thr_ref[...], need_ref[...], run_ref, utri_ref[...], ones_ref[...])
        lg = _dot(qs_ref[...].astype(BF16), kt)
        parts = [lg[h * rq:(h + 1) * rq] + sel + tile_of(h) for h in range(nh)]
        _flash_update_rows(jnp.concatenate(parts, axis=0), vt, m_ref, l_ref, acc_ref)

    @pl.when(s >= ns)
    def _():
        is_last = s == 2 * ns - 1
        attend(keys_ref[s - ns], _cat_pages(kv_refs, 0, LANES), _cat_pages(kv_refs, LANES, 2 * LANES),
               lambda h: _last_page_tile(slast_ref[h], is_last))

    @pl.when(s == 2 * ns - 1)
    def _():
        kvn = kvnew_ref[...]
        ktn = _pad_rows(kvn[:, :LANES], pad_ref, True)
        vtn = _pad_rows(kvn[:, LANES:], pad_ref, True)
        attend(keysn_ref[...], ktn, vtn, lambda h: snew_ref[h])
        inv = 1.0 / l_ref[...]
        for h in range(nh):
            rows = slice(h * rq, (h + 1) * rq)
            o_ref[:, h * LANES:(h + 1) * LANES] = acc_ref[rows, :] * inv[rows]


def _dsa_sample(m, kvnew, cache_idx, cache_kv, page_table, slast, snew, db):
    n_pages = page_table.shape[1]
    ns = n_pages // PAGES_PER_STEP
    assert ns * PAGES_PER_STEP == n_pages
    k_top = min(IDX_TOPK, (n_pages * PAGE + 4) // 4)
    row = lambda width: pl.BlockSpec((DS_PAD, width), lambda b, s, pt: (b, 0))
    const = lambda shape: pl.BlockSpec(shape, lambda b, s, pt: (0,) * len(shape))
    rows = MIX_HEADS * DS_PAD
    grid_spec = pltpu.PrefetchScalarGridSpec(
        num_scalar_prefetch=1,
        grid=(db, 2 * ns),
        in_specs=[row(1024), row(1024), row(LANES), row(LANES), row(2 * LANES)]
        + _page_specs((HEAD_DIM, PAGE), lambda s: jnp.minimum(s, ns - 1))
        + _page_specs((2 * LANES, PAGE), lambda s: jnp.maximum(s - ns, 0))
        + [const((MIX_HEADS, DS_PAD, PAGE)), const((MIX_HEADS, DS_PAD, PAGE)), const((PAGE, PAGE)),
           const((PAGE, PAGE))],
        out_specs=row(1024),
        scratch_shapes=[pltpu.VMEM((ns, DS_PAD, STEP_KEYS), I32), pltpu.VMEM((DS_PAD, PAGE), I32),
                        pltpu.VMEM((rows, LANES), F32), pltpu.VMEM((rows, LANES), F32),
                        pltpu.VMEM((IDX_HEADS, DS_PAD, LANES), F32),
                        pltpu.VMEM((DS_PAD, LANES), I32), pltpu.VMEM((DS_PAD, LANES), F32),
                        pltpu.VMEM((DS_PAD, LANES), F32), pltpu.VMEM((PAGE, LANES), F32),
                        pltpu.VMEM((rows, 1), F32), pltpu.VMEM((rows, 1), F32),
                        pltpu.VMEM((rows, LANES), F32)])
    idx_pages = _feature_major_pages(cache_idx)
    kv_pages = _feature_major_pages(cache_kv)
    return pl.pallas_call(
        functools.partial(_dsa_sample_kernel, n_pages=n_pages, k_top=k_top),
        grid_spec=grid_spec,
        out_shape=jax.ShapeDtypeStruct((db * DS_PAD, 1024), F32),
        compiler_params=_cparams("arbitrary", "arbitrary"),
        name="dsa_sample",
    )(page_table, m["qd"], m["iq"], m["iw"], m["ika"].astype(F32), kvnew,
      *([idx_pages] * PAGES_PER_STEP), *([kv_pages] * PAGES_PER_STEP), slast, snew,
      _tri(PAGE, False), jnp.ones((PAGE, PAGE), BF16))


def _compress_sample_kernel(pt_ref, *rest, n_pages):
    npp = PAGES_PER_STEP
    page_refs = rest[:npp]
    wbig_ref, gck_ref, s64_ref, ck_ref, cv_ref, x_ref, nat_ref = rest[npp:]
    s = pl.program_id(1)
    for pp in range(npp):
        r0 = pl.multiple_of((s * npp + pp) * PAGE, PAGE)
        for j in range(2):
            x_ref[j, pl.ds(r0, PAGE), :] = page_refs[pp][0, j * LANES:(j + 1) * LANES, :].T

    @pl.when(s == n_pages // npp - 1)
    def _():
        n_cmp = n_pages * PAGE // CMP_BLOCK
        width = 2 * LANES
        acc = jnp.zeros((n_cmp, width), F32)
        for l in range(CMP_BLOCK):
            rows = pl.ds(l, n_cmp, stride=CMP_BLOCK)
            xl = jnp.concatenate([x_ref[0, rows, :], x_ref[1, rows, :]], axis=1).astype(BF16)
            acc = acc + _dot(xl, wbig_ref[l * width:(l + 1) * width, :])
        nat_ref[0] = acc[:, :LANES]
        nat_ref[1] = acc[:, LANES:]
        half = n_cmp // 2
        for p in range(2):
            rows = pl.ds(p, half, stride=2)
            ck_ref[0, p * half:(p + 1) * half, :] = _norm64(nat_ref[0, rows, :], gck_ref[...],
                                                            s64_ref[...]).astype(BF16)
            cv_ref[0, p * half:(p + 1) * half, :] = nat_ref[1, rows, :].astype(BF16)


def _compress_sample(cache_cmp, page_table, cw, db):
    wbig, gck = cw
    n_pages = page_table.shape[1]
    n_cmp = n_pages * PAGE // CMP_BLOCK
    const = lambda shape: pl.BlockSpec(shape, lambda b, s, pt: (0,) * len(shape))
    out = jax.ShapeDtypeStruct((db, n_cmp, LANES), BF16)
    grid_spec = pltpu.PrefetchScalarGridSpec(
        num_scalar_prefetch=1,
        grid=(db, n_pages // PAGES_PER_STEP),
        in_specs=_page_specs((2 * LANES, PAGE), lambda s: s)
        + [const(wbig.shape), const((1, LANES)), const((LANES, LANES))],
        out_specs=[pl.BlockSpec((1, n_cmp, LANES), lambda b, s, pt: (b, 0, 0))] * 2,
        scratch_shapes=[pltpu.VMEM((2, n_pages * PAGE, LANES), F32), pltpu.VMEM((2, n_cmp, LANES), F32)])
    return pl.pallas_call(
        functools.partial(_compress_sample_kernel, n_pages=n_pages),
        grid_spec=grid_spec,
        out_shape=[out, out],
        compiler_params=_cparams("arbitrary", "arbitrary"),
        name="compress_sample",
    )(page_table, *([_feature_major_pages(cache_cmp)] * PAGES_PER_STEP), wbig, gck, _seg_matrix())


def _nsa_sample_kernel(pt_ref, qa_ref, ng_ref, ck_ref, cv_ref, win_ref, sknew_ref, wknew_ref, *rest,
                       n_pages, n_sel):
    npp = PAGES_PER_STEP
    sel_refs = rest[:npp]
    (scmp_ref, slast_ref, snew_ref, swin_ref, exp_ref, o_ref,
     qs_ref, picked_ref, oc_ref, pad_ref, m_ref, l_ref, acc_ref) = rest[npp:]
    s = pl.program_id(1)
    ns = n_pages // npp
    nh = MIX_HEADS
    gh = GROUP_HEADS
    rq = DS_PAD
    n_cmp = ck_ref.shape[1]
    n_past_blk = n_cmp // 2
    nb = picked_ref.shape[2]

    def tiled(lg, tile_of, extra=None):
        parts = []
        for h in range(nh):
            part = lg[h * rq:(h + 1) * rq] + tile_of(h)
            if extra is not None:
                part = part + extra[h // gh]
            parts.append(part)
        return jnp.concatenate(parts, axis=0)

    @pl.when(s == 0)
    def _():
        _stack_heads(qa_ref[...], qs_ref, nh, rq)
        lc = tiled(_dot_nt(qs_ref[...].astype(BF16), ck_ref[0]), lambda h: scmp_ref[h])
        pc = _softmax_valid(lc, 1)
        oc_ref[...] = _dot(pc.astype(BF16), cv_ref[0])
        blk = _iota2((rq, nb), 1)
        forced = (blk == 0) | (blk == n_past_blk) | (blk == n_past_blk - 1)
        for g in range(N_GROUPS):
            imp = pc[g * gh * rq:(g * gh + 1) * rq]
            for r in range(1, gh):
                imp = imp + pc[(g * gh + r) * rq:(g * gh + r + 1) * rq]
            imp = imp[:, :n_past_blk] + imp[:, n_past_blk:]
            imp = jnp.concatenate([imp, jnp.zeros((rq, nb - n_past_blk), F32)], axis=1)
            score = jnp.where(forced, BIG, jnp.where(blk < n_past_blk, imp, -jnp.inf))
            picked_ref[g] = _pick_blocks(score, n_sel, 1)
        _flash_init(m_ref, l_ref, acc_ref)

    keep = [jnp.where(_dot(picked_ref[g].astype(BF16), exp_ref[s]) > 0.5, 0.0, NEG) for g in range(N_GROUPS)]
    lg = _dot(qs_ref[...].astype(BF16), _cat_pages(sel_refs, 0, LANES))
    is_last = s == ns - 1
    _flash_update_rows(tiled(lg, lambda h: _last_page_tile(slast_ref[h], is_last), keep),
                       _cat_pages(sel_refs, LANES, 2 * LANES), m_ref, l_ref, acc_ref)

    @pl.when(s == ns - 1)
    def _():
        ng = ng_ref[...]
        new_tile = lambda h: snew_ref[h]
        skn = sknew_ref[...]
        ktn = _pad_rows(skn[:, :LANES], pad_ref, True)
        vtn = _pad_rows(skn[:, LANES:], pad_ref, True)
        _flash_update_rows(tiled(_dot(qs_ref[...].astype(BF16), ktn), new_tile), vtn, m_ref, l_ref, acc_ref)
        o_s = acc_ref[...] * (1.0 / l_ref[...])
        _flash_init(m_ref, l_ref, acc_ref)
        lg = _dot(qs_ref[...].astype(BF16), win_ref[0, :LANES, :].astype(BF16))
        _flash_update_rows(tiled(lg, lambda h: swin_ref[h]), win_ref[0, LANES:, :].astype(BF16),
                           m_ref, l_ref, acc_ref)
        wkn = wknew_ref[...]
        ktn = _pad_rows(wkn[:, :LANES], pad_ref, True)
        vtn = _pad_rows(wkn[:, LANES:], pad_ref, True)
        _flash_update_rows(tiled(_dot(qs_ref[...].astype(BF16), ktn), new_tile), vtn, m_ref, l_ref, acc_ref)
        o_w = acc_ref[...] * (1.0 / l_ref[...])
        o_c = oc_ref[...]
        for h in range(nh):
            sl = slice(h * rq, (h + 1) * rq)
            gate = lambda j: ng[:, MISC_NG + 3 * h + j:MISC_NG + 3 * h + j + 1]
            o_ref[:, h * LANES:(h + 1) * LANES] = gate(0) * o_c[sl] + gate(1) * o_s[sl] + gate(2) * o_w[sl]


def _sample_block_expander(n_pages, nb):
    key_blk = np.arange(n_pages * PAGE) // SEL_BLOCK
    e = (np.arange(nb)[:, None] == key_blk[None, :]).astype(np.float32)
    return jnp.asarray(e.reshape(nb, n_pages // PAGES_PER_STEP, STEP_KEYS).transpose(1, 0, 2), BF16)


def _nsa_sample(m, sknew, wknew, ck, cv, cache_sel, win_t, page_table, scmp, slast, snew, swin, db):
    n_pages = page_table.shape[1]
    ns = n_pages // PAGES_PER_STEP
    n_cmp = ck.shape[1]
    n_sblk = n_cmp // 2 + 1
    n_sel = min(N_SEL, n_sblk)
    nb = -(-n_sblk // LANES) * LANES
    assert win_t.shape[2] == WINDOW
    row = lambda width: pl.BlockSpec((DS_PAD, width), lambda b, s, pt: (b, 0))
    seq = lambda r, width: pl.BlockSpec((1, r, width), lambda b, s, pt: (b, 0, 0))
    const = lambda shape: pl.BlockSpec(shape, lambda b, s, pt: (0,) * len(shape))
    rows = MIX_HEADS * DS_PAD
    grid_spec = pltpu.PrefetchScalarGridSpec(
        num_scalar_prefetch=1,
        grid=(db, ns),
        in_specs=[row(1024), row(LANES), seq(n_cmp, LANES), seq(n_cmp, LANES), seq(2 * LANES, WINDOW),
                  row(2 * LANES), row(2 * LANES)]
        + _page_specs((2 * LANES, PAGE), lambda s: s)
        + [const((MIX_HEADS, DS_PAD, n_cmp)), const((MIX_HEADS, DS_PAD, PAGE)),
           const((MIX_HEADS, DS_PAD, PAGE)), const((MIX_HEADS, DS_PAD, WINDOW)), const((ns, nb, STEP_KEYS))],
        out_specs=row(1024),
        scratch_shapes=[pltpu.VMEM((rows, LANES), F32), pltpu.VMEM((N_GROUPS, DS_PAD, nb), F32),
                        pltpu.VMEM((rows, LANES), F32), pltpu.VMEM((PAGE, LANES), F32),
                        pltpu.VMEM((rows, 1), F32), pltpu.VMEM((rows, 1), F32),
                        pltpu.VMEM((rows, LANES), F32)])
    return pl.pallas_call(
        functools.partial(_nsa_sample_kernel, n_pages=n_pages, n_sel=n_sel),
        grid_spec=grid_spec,
        out_shape=jax.ShapeDtypeStruct((db * DS_PAD, 1024), F32),
        compiler_params=_cparams("arbitrary", "arbitrary"),
        name="nsa_sample",
    )(page_table, m["qn"], m["ng"], ck, cv, win_t, sknew, wknew,
      *([_feature_major_pages(cache_sel)] * PAGES_PER_STEP), scmp, slast, snew, swin,
      _sample_block_expander(n_pages, nb))


def _kv_leaf(t):
    nb, _, s = t.shape
    return t.reshape(nb, 2, N_GROUPS, HEAD_DIM, s).transpose(0, 4, 1, 2, 3)


def kernel(x_prompt, x_sample, cache_dsa_kv, cache_dsa_idx_k, cache_nsa_cmp_kv, cache_nsa_sel_kv,
           state_nsa_win_kv, page_table, rel_bias, ffn1_norm, ffn1_w_gu, ffn1_w_down, mix_norm, w_in,
           dsa_q_norm, dsa_k_norm, idx_k_norm, nsa_q_norm, nsa_cmp_k_norm, nsa_sel_k_norm, nsa_win_k_norm,
           nsa_w_phi, w_dsa_o, w_nsa_o, w_out, ffn2_norm, ffn2_w_gu, ffn2_w_down):
    b, s, d = x_prompt.shape
    db, ds, _ = x_sample.shape
    depth = ffn1_norm.shape[0]
    n_pages = page_table.shape[1]
    past = n_pages * PAGE
    assert s % TQ == 0 and ds <= DS_PAD and ds < CMP_BLOCK and past >= WINDOW
    n_p, n_s = b * s, db * DS_PAD
    tm_p, tm_s = min(512, s), min(512, n_s)
    tm_ffn = min(1024, n_p)

    hp = x_prompt.reshape(n_p, d)
    hs = jnp.pad(x_sample, ((0, 0), (0, DS_PAD - ds), (0, 0))).reshape(n_s, d)
    dsa_h = slice(0, MIX_HEADS)
    nsa_h = slice(MIX_HEADS, 2 * MIX_HEADS)
    tdiag, tprev, slast, snew, swin, scmp = _bias_tiles(rel_bias, past // CMP_BLOCK)
    tcmp = _cmp_tiles(rel_bias, MIX_HEADS, s // TQ, s // CMP_BLOCK)

    outs_p = [[] for _ in range(5)]
    outs_s = [[] for _ in range(5)]
    for l in range(depth):
        pw = _proj_weights(w_in[l], dsa_q_norm[l], dsa_k_norm[l], idx_k_norm[l], nsa_q_norm[l],
                           nsa_sel_k_norm[l], nsa_win_k_norm[l])
        cw = _compress_weights(nsa_w_phi[l], nsa_cmp_k_norm[l])
        wa, wb = _pad_head_rows(w_dsa_o[l]), _pad_head_rows(w_nsa_o[l])
        wo = w_out[l].astype(BF16)

        hp = _ffn(hp, ffn1_norm[l], ffn1_w_gu[l], ffn1_w_down[l], tm_ffn)
        mp = _proj(hp, mix_norm[l], pw, b, tm_p, BF16, True)
        oa = _dsa_prompt(mp, tdiag[dsa_h], tprev[dsa_h], b, s)
        ck, cvt = _compress_prompt(mp["ckv"], cw, b, s)
        ob = _nsa_prompt(mp, ck, cvt, tcmp, tdiag[nsa_h], tprev[nsa_h], b, s)
        hp = _merge(hp, oa.reshape(n_p, -1), ob.reshape(n_p, -1), mp["ga"], mp["gb"], wa, wb, wo, tm_p)
        hp = _ffn(hp, ffn2_norm[l], ffn2_w_gu[l], ffn2_w_down[l], tm_ffn)
        win_len = min(WINDOW, s)
        for lst, val in zip(outs_p, (_kv_leaf(mp["dkvt"]), mp["ikt"].transpose(0, 2, 1), _kv_leaf(mp["ckvt"]),
                                     _kv_leaf(mp["skvt"]), _kv_leaf(mp["wkvt"][:, :, s - win_len:]))):
            lst.append(val)

        hs = _ffn(hs, ffn1_norm[l], ffn1_w_gu[l], ffn1_w_down[l], tm_s)
        ms = _proj(hs, mix_norm[l], pw, 1, tm_s, F32, False)
        new_rows = lambda name: ms[name][0].T
        new_t = lambda name: ms[name][0].reshape(-1, db, DS_PAD)[:, :, :ds].transpose(1, 0, 2)
        oa_s = _dsa_sample(ms, new_rows("dkvt"), cache_dsa_idx_k[l], cache_dsa_kv[l], page_table,
                           slast[dsa_h], snew[dsa_h], db)
        ck_s, cv_s = _compress_sample(cache_nsa_cmp_kv[l], page_table, cw, db)
        win_t = _feature_major_pages(state_nsa_win_kv[l])
        ob_s = _nsa_sample(ms, new_rows("skvt"), new_rows("wkvt"), ck_s, cv_s, cache_nsa_sel_kv[l], win_t,
                           page_table, scmp[nsa_h], slast[nsa_h], snew[nsa_h], swin[nsa_h], db)
        hs = _merge(hs, oa_s.astype(BF16), ob_s.astype(BF16), ms["ga"], ms["gb"], wa, wb, wo, tm_s)
        hs = _ffn(hs, ffn2_norm[l], ffn2_w_gu[l], ffn2_w_down[l], tm_s)
        win_new = jnp.concatenate([win_t, new_t("wkvt")], axis=2)
        win_new = win_new[:, :, win_new.shape[2] - min(WINDOW, past + ds):]
        for lst, val in zip(outs_s, (_kv_leaf(new_t("dkvt")), new_t("ikt").transpose(0, 2, 1),
                                     _kv_leaf(new_t("ckvt")), _kv_leaf(new_t("skvt")), _kv_leaf(win_new))):
            lst.append(val)

    y_p = hp.reshape(b, s, d)
    y_s = hs.reshape(db, DS_PAD, d)[:, :ds]
    return (y_p, y_s) + tuple(jnp.stack(o) for o in outs_p) + tuple(jnp.stack(o) for o in outs_s)
```
